```python
import jax, jax.numpy as jnp
from jax import lax
import numpy as np

D_MODEL = 1024
BATCH = 2
SEQ = 16384
DEPTH = 2

POOL_GROUPS = 4
POOL_GROUP_DIM = 64
POOL_WINDOWS = (2, 4, 8, 16)
POOL_DIM = POOL_GROUPS * POOL_GROUP_DIM
ATT_HEADS = 8
HEAD_DIM = 64
ATT_DIM = ATT_HEADS * HEAD_DIM
MOBA_BLOCK = 256
MOBA_TOPK = 3
MOBA_Q_CHUNK = 64
ROPE_THETA = 10000.0
SGU_GROUPS = 4
SGU_GROUP_DIM = 64
SGU_DIM = SGU_GROUPS * SGU_GROUP_DIM
SGU_CHUNK = 128
CONV_DIM = 256
CONV_WIDTH = 31
N_BRANCH = 4
FF_DIM = -(-8 * D_MODEL // (3 * 256)) * 256
IN_DIM = POOL_DIM + 3 * ATT_DIM + 2 * SGU_DIM + 2 * CONV_DIM
EPS = 1e-6
NEG = -1e30

kernel_name = 'hybrid_pool_moba_sgu_conv_gated_block'


def rms_norm(x, g):
    xf = x.astype(jnp.float32)
    y = xf * lax.rsqrt(jnp.mean(xf * xf, axis=-1, keepdims=True) + EPS)
    return (y * g.astype(jnp.float32)).astype(x.dtype)


def layer_norm(x, g, b):
    xf = x.astype(jnp.float32)
    mu = jnp.mean(xf, axis=-1, keepdims=True)
    var = jnp.mean(jnp.square(xf - mu), axis=-1, keepdims=True)
    y = (xf - mu) * lax.rsqrt(var + EPS)
    return (y * g.astype(jnp.float32) + b.astype(jnp.float32)).astype(x.dtype)


def rope(x, positions):
    half = HEAD_DIM // 2
    inv_freq = 1.0 / (ROPE_THETA ** (jnp.arange(half, dtype=jnp.float32) / half))
    ang = positions.astype(jnp.float32)[..., None] * inv_freq
    cos = jnp.cos(ang)[:, :, None, :]
    sin = jnp.sin(ang)[:, :, None, :]
    xf = x.astype(jnp.float32)
    x1, x2 = xf[..., :half], xf[..., half:]
    out = jnp.concatenate([x1 * cos - x2 * sin, x2 * cos + x1 * sin], axis=-1)
    return out.astype(x.dtype)


def pool_mixer(xa, pool_w, pool_scale):
    B, S, _ = xa.shape
    xf = xa.astype(jnp.float32).reshape(B, S, POOL_GROUPS, POOL_GROUP_DIM)
    cs = jnp.cumsum(xf, axis=1)
    t = jnp.arange(S)
    outs = []
    for g, w in enumerate(POOL_WINDOWS):
        c = cs[:, :, g]
        lagged = jnp.pad(c, ((0, 0), (w, 0), (0, 0)))[:, :S]
        cnt = jnp.minimum(t + 1, w).astype(jnp.float32)[None, :, None]
        outs.append((c - lagged) / cnt - xf[:, :, g])
    pooled = jnp.stack(outs, axis=2)
    mixed = jnp.einsum('bsgc,gcd->bsgd', pooled, pool_w.astype(jnp.float32))
    return (mixed.reshape(B, S, POOL_DIM) * pool_scale.astype(jnp.float32)).astype(xa.dtype)


def moba_attention(q, k, v):
    B, S, H, Dh = q.shape
    s_pad = -(-S // MOBA_BLOCK) * MOBA_BLOCK
    pad = ((0, 0), (0, s_pad - S), (0, 0), (0, 0))
    q, k, v = jnp.pad(q, pad), jnp.pad(k, pad), jnp.pad(v, pad)
    nb = s_pad // MOBA_BLOCK
    nq = s_pad // MOBA_Q_CHUNK
    top = min(MOBA_TOPK, nb)
    scale = HEAD_DIM ** -0.5
    kb = k.transpose(0, 2, 1, 3).reshape(B, H, nb, MOBA_BLOCK, Dh)
    vb = v.transpose(0, 2, 1, 3).reshape(B, H, nb, MOBA_BLOCK, Dh)
    kmean = jnp.mean(kb.astype(jnp.float32), axis=3)
    q_chunks = q.transpose(0, 2, 1, 3).reshape(B, H, nq, MOBA_Q_CHUNK, Dh).transpose(2, 0, 1, 3, 4)
    b_idx = jnp.arange(B)[:, None, None, None]
    h_idx = jnp.arange(H)[None, :, None, None]
    blk_ids = jnp.arange(nb)
    key_off = jnp.arange(MOBA_BLOCK)
    q_off = jnp.arange(MOBA_Q_CHUNK)

    def one_chunk(args):
        c, qc = args
        qf = qc.astype(jnp.float32)
        blk = (c * MOBA_Q_CHUNK) // MOBA_BLOCK
        gate = jnp.einsum('bhqd,bhnd->bhqn', qf, kmean)
        gate = jnp.where(blk_ids < blk, gate, NEG)
        _, sel = lax.top_k(gate, top)
        sel_valid = sel < blk
        ks = kb[b_idx, h_idx, sel].astype(jnp.float32)
        vs = vb[b_idx, h_idx, sel].astype(jnp.float32)
        s_sel = jnp.einsum('bhqd,bhqnkd->bhqnk', qf, ks) * scale
        s_sel = jnp.where(sel_valid[..., None], s_sel, NEG).reshape(B, H, MOBA_Q_CHUNK, top * MOBA_BLOCK)
        k_own = lax.dynamic_index_in_dim(kb, blk, axis=2, keepdims=False).astype(jnp.float32)
        v_own = lax.dynamic_index_in_dim(vb, blk, axis=2, keepdims=False).astype(jnp.float32)
        s_own = jnp.einsum('bhqd,bhkd->bhqk', qf, k_own) * scale
        q_pos = c * MOBA_Q_CHUNK + q_off
        k_pos = blk * MOBA_BLOCK + key_off
        s_own = jnp.where(k_pos[None, :] <= q_pos[:, None], s_own, NEG)
        p = jax.nn.softmax(jnp.concatenate([s_sel, s_own], axis=-1), axis=-1)
        p_sel = p[..., :top * MOBA_BLOCK].reshape(B, H, MOBA_Q_CHUNK, top, MOBA_BLOCK)
        p_own = p[..., top * MOBA_BLOCK:]
        o = jnp.einsum('bhqnk,bhqnkd->bhqd', p_sel, vs) + jnp.einsum('bhqk,bhkd->bhqd', p_own, v_own)
        return o.astype(qc.dtype)

    out = lax.map(one_chunk, (jnp.arange(nq), q_chunks))
    out = out.transpose(1, 0, 3, 2, 4).reshape(B, s_pad, H * Dh)
    return out[:, :S]


def spatial_gating(z, norm_g, norm_b, w_s, b_s):
    B, S, _ = z.shape
    u, v = z[..., :SGU_DIM], z[..., SGU_DIM:]
    v = layer_norm(v, norm_g, norm_b)
    nc = S // SGU_CHUNK
    vc = v.astype(jnp.float32).reshape(B, nc, SGU_CHUNK, SGU_GROUPS, SGU_GROUP_DIM)
    mask = jnp.tril(jnp.ones((SGU_CHUNK, SGU_CHUNK), dtype=bool))
    ws = jnp.where(mask[None], w_s.astype(jnp.float32), 0.0)
    f = jnp.einsum('gts,bnsgc->bntgc', ws, vc) + b_s.astype(jnp.float32).T[None, None, :, :, None]
    return (u.astype(jnp.float32) * f.reshape(B, S, SGU_DIM)).astype(z.dtype)


def conv_module(xd, conv_w, conv_b, norm_g, norm_b):
    a, g = xd[..., :CONV_DIM], xd[..., CONV_DIM:]
    h = a * jax.nn.sigmoid(g)
    h = lax.conv_general_dilated(
        h, conv_w[:, None, :], window_strides=(1,), padding=[(CONV_WIDTH - 1, 0)],
        dimension_numbers=('NWC', 'WIO', 'NWC'), feature_group_count=CONV_DIM) + conv_b
    return jax.nn.silu(layer_norm(h, norm_g, norm_b))


def setup_inputs(seed: int = 0) -> dict:
    key = jax.random.key(seed)
    ks = jax.random.split(key, 32)
    L, D = DEPTH, D_MODEL

    def nrm(k, shape, scale):
        return jax.random.normal(k, shape, jnp.float32) * scale

    x = nrm(ks[0], (BATCH, SEQ, D), 1.0)
    positions = (jax.random.randint(ks[1], (BATCH, 1), 0, 1024) + jnp.arange(SEQ)[None, :]).astype(jnp.int32)
    return {
        'x': x,
        'positions': positions,
        'norm_mix_g': 1.0 + nrm(ks[2], (L, D), 0.05),
        'w_in': nrm(ks[3], (L, D, IN_DIM), D ** -0.5),
        'w_gate': nrm(ks[4], (L, D, N_BRANCH * D), D ** -0.5),
        'b_gate': nrm(ks[5], (L, N_BRANCH * D), 0.1),
        'pool_w': nrm(ks[6], (L, POOL_GROUPS, POOL_GROUP_DIM, POOL_GROUP_DIM), POOL_GROUP_DIM ** -0.5),
        'pool_scale': 1.0 + nrm(ks[7], (L, POOL_DIM), 0.1),
        'sgu_norm_g': 1.0 + nrm(ks[8], (L, SGU_DIM), 0.05),
        'sgu_norm_b': nrm(ks[9], (L, SGU_DIM), 0.02),
        'sgu_w': nrm(ks[10], (L, SGU_GROUPS, SGU_CHUNK, SGU_CHUNK), SGU_CHUNK ** -0.5),
        'sgu_b': 1.0 + nrm(ks[11], (L, SGU_GROUPS, SGU_CHUNK), 0.1),
        'conv_w': nrm(ks[12], (L, CONV_WIDTH, CONV_DIM), CONV_WIDTH ** -0.5),
        'conv_b': nrm(ks[13], (L, CONV_DIM), 0.02),
        'conv_norm_g': 1.0 + nrm(ks[14], (L, CONV_DIM), 0.05),
        'conv_norm_b': nrm(ks[15], (L, CONV_DIM), 0.02),
        'w_proj_a': nrm(ks[16], (L, POOL_DIM, D), POOL_DIM ** -0.5),
        'w_proj_b': nrm(ks[17], (L, ATT_DIM, D), ATT_DIM ** -0.5),
        'w_proj_c': nrm(ks[18], (L, SGU_DIM, D), SGU_DIM ** -0.5),
        'w_proj_d': nrm(ks[19], (L, CONV_DIM, D), CONV_DIM ** -0.5),
        'w_o': nrm(ks[20], (L, D, D), D ** -0.5),
        'norm_ffn_g': 1.0 + nrm(ks[21], (L, D), 0.05),
        'w_ffn_in': nrm(ks[22], (L, D, 2 * FF_DIM), D ** -0.5),
        'w_ffn_out': nrm(ks[23], (L, FF_DIM, D), FF_DIM ** -0.5),
        'final_norm_g': 1.0 + nrm(ks[24], (D,), 0.05),
    }


def reference(x, positions, norm_mix_g, w_in, w_gate, b_gate, pool_w, pool_scale,
              sgu_norm_g, sgu_norm_b, sgu_w, sgu_b, conv_w, conv_b, conv_norm_g, conv_norm_b,
              w_proj_a, w_proj_b, w_proj_c, w_proj_d, w_o, norm_ffn_g, w_ffn_in, w_ffn_out,
              final_norm_g):
    B, S, D = x.shape
    splits = np.cumsum([POOL_DIM, ATT_DIM, ATT_DIM, ATT_DIM, 2 * SGU_DIM]).tolist()
    for l in range(DEPTH):
        xn = rms_norm(x, norm_mix_g[l])
        proj = xn @ w_in[l]
        xa, q, k, v, zc, xd = jnp.split(proj, splits, axis=-1)
        q = rope(q.reshape(B, S, ATT_HEADS, HEAD_DIM), positions)
        k = rope(k.reshape(B, S, ATT_HEADS, HEAD_DIM), positions)
        v = v.reshape(B, S, ATT_HEADS, HEAD_DIM)
        ya = pool_mixer(xa, pool_w[l], pool_scale[l])
        yb = moba_attention(q, k, v)
        yc = spatial_gating(jax.nn.gelu(zc), sgu_norm_g[l], sgu_norm_b[l], sgu_w[l], sgu_b[l])
        yd = conv_module(xd, conv_w[l], conv_b[l], conv_norm_g[l], conv_norm_b[l])
        gates = jax.nn.sigmoid((xn @ w_gate[l] + b_gate[l]).astype(jnp.float32)).reshape(B, S, N_BRANCH, D)
        merged = (gates[:, :, 0] * (ya @ w_proj_a[l]) + gates[:, :, 1] * (yb @ w_proj_b[l])
                  + gates[:, :, 2] * (yc @ w_proj_c[l]) + gates[:, :, 3] * (yd @ w_proj_d[l]))
        x = x + merged.astype(x.dtype) @ w_o[l]
        hn = rms_norm(x, norm_ffn_g[l])
        gu = hn @ w_ffn_in[l]
        x = x + (jax.nn.silu(gu[..., :FF_DIM]) * gu[..., FF_DIM:]) @ w_ffn_out[l]
    return rms_norm(x, final_norm_g)
```

```python
import functools

import jax
import jax.numpy as jnp
from jax import lax
from jax.experimental import pallas as pl
from jax.experimental.pallas import tpu as pltpu

F32 = jnp.float32
BF16 = jnp.bfloat16

D_MODEL = 1024
POOL_GROUPS = 4
POOL_GROUP_DIM = 64
POOL_WINDOWS = (2, 4, 8, 16)
POOL_DIM = POOL_GROUPS * POOL_GROUP_DIM
ATT_HEADS = 8
HEAD_DIM = 64
HALF_DIM = HEAD_DIM // 2
ATT_DIM = ATT_HEADS * HEAD_DIM
MOBA_BLOCK = 256
MOBA_TOPK = 3
ROPE_THETA = 10000.0
SGU_GROUPS = 4
SGU_GROUP_DIM = 64
SGU_DIM = SGU_GROUPS * SGU_GROUP_DIM
SGU_CHUNK = 128
CONV_DIM = 256
CONV_WIDTH = 31
N_BRANCH = 4
FF_DIM = 2816
EPS = 1e-6
NEG = -1e30
BELOW_NEG = -3e38

HALO = 32
TOKEN_TILE = 512
FF_CHUNK = 1408
AUG = 2 * HEAD_DIM
VMEM_LIMIT = 56 * 1024 * 1024


def _rms(x, g):
    return x * lax.rsqrt(jnp.mean(x * x, axis=-1, keepdims=True) + EPS) * g


def _layer_norm(x, g, b):
    mu = jnp.mean(x, axis=-1, keepdims=True)
    xc = x - mu
    var = jnp.mean(xc * xc, axis=-1, keepdims=True)
    return xc * lax.rsqrt(var + EPS) * g + b


def _const_spec(shape):
    nd = len(shape)
    return pl.BlockSpec(shape, lambda *_: (0,) * nd, pipeline_mode=pl.Buffered(1))


def _rope_table_kernel(pos_ref, invf_ref, cos_ref, sin_ref):
    ang = invf_ref[...] * pos_ref[...]
    cos_ref[...] = jnp.cos(ang)
    sin_ref[...] = jnp.sin(ang)


def _rope_tables(pos_row, invf_col):
    n = pos_row.shape[1]
    tn = 2048
    return pl.pallas_call(
        _rope_table_kernel,
        out_shape=(jax.ShapeDtypeStruct((HALF_DIM, n), F32),) * 2,
        grid=(n // tn,),
        in_specs=[pl.BlockSpec((1, tn), lambda i: (0, i)),
                  pl.BlockSpec((HALF_DIM, 1), lambda i: (0, 0))],
        out_specs=(pl.BlockSpec((HALF_DIM, tn), lambda i: (0, i)),) * 2,
        name="rope_tables",
    )(pos_row, invf_col)


def _inproj_kernel(x_ref, g_ref, wqkvT_ref, wrest_ref, cos_ref, sin_ref, sgug_ref, sgub_ref,
                   qT_ref, kT_ref, vT_ref, xa_ref, zu_ref, zv_ref, hc_ref):
    xn = _rms(x_ref[...], g_ref[...]).astype(BF16)
    pT = lax.dot_general(wqkvT_ref[...], xn, (((1,), (1,)), ((), ())),
                         preferred_element_type=F32)
    c = cos_ref[...]
    s = sin_ref[...]
    for base, out_ref in ((0, qT_ref), (ATT_DIM, kT_ref)):
        for h in range(ATT_HEADS):
            r0 = base + h * HEAD_DIM
            x1 = pT[r0:r0 + HALF_DIM]
            x2 = pT[r0 + HALF_DIM:r0 + HEAD_DIM]
            o0 = h * HEAD_DIM
            out_ref[o0:o0 + HALF_DIM, :] = x1 * c - x2 * s
            out_ref[o0 + HALF_DIM:o0 + HEAD_DIM, :] = x2 * c + x1 * s
    vT_ref[...] = pT[2 * ATT_DIM:].astype(BF16)

    r = jnp.dot(xn, wrest_ref[...], preferred_element_type=F32)
    xa_ref[...] = r[:, :POOL_DIM]
    o = POOL_DIM
    zu_ref[...] = jax.nn.gelu(r[:, o:o + SGU_DIM], approximate=True)
    zv = jax.nn.gelu(r[:, o + SGU_DIM:o + 2 * SGU_DIM], approximate=True)
    zv_ref[...] = _layer_norm(zv, sgug_ref[...], sgub_ref[...]).astype(BF16)
    o += 2 * SGU_DIM
    hc_ref[...] = r[:, o:o + CONV_DIM] * jax.nn.sigmoid(r[:, o + CONV_DIM:o + 2 * CONV_DIM])


def _inproj(x2d, g, wqkvT, wrest, cosT, sinT, sgug, sgub):
    n = x2d.shape[0]
    tm = TOKEN_TILE
    n_rest = wrest.shape[1]
    tok = lambda w: pl.BlockSpec((tm, w), lambda i: (i, 0))
    feat = lambda r: pl.BlockSpec((r, tm), lambda i: (0, i))
    return pl.pallas_call(
        _inproj_kernel,
        out_shape=(jax.ShapeDtypeStruct((ATT_DIM, n), F32),
                   jax.ShapeDtypeStruct((ATT_DIM, n), F32),
                   jax.ShapeDtypeStruct((ATT_DIM, n), BF16),
                   jax.ShapeDtypeStruct((n, POOL_DIM), F32),
                   jax.ShapeDtypeStruct((n, SGU_DIM), F32),
                   jax.ShapeDtypeStruct((n, SGU_DIM), BF16),
                   jax.ShapeDtypeStruct((n, CONV_DIM), F32)),
        grid=(n // tm,),
        in_specs=[tok(D_MODEL), _const_spec((1, D_MODEL)),
                  _const_spec((3 * ATT_DIM, D_MODEL)), _const_spec((D_MODEL, n_rest)),
                  feat(HALF_DIM), feat(HALF_DIM),
                  _const_spec((1, SGU_DIM)), _const_spec((1, SGU_DIM))],
        out_specs=(feat(ATT_DIM), feat(ATT_DIM), feat(ATT_DIM),
                   tok(POOL_DIM), tok(SGU_DIM), tok(SGU_DIM), tok(CONV_DIM)),
        compiler_params=pltpu.CompilerParams(dimension_semantics=("parallel",),
                                             vmem_limit_bytes=VMEM_LIMIT),
        name="inproj",
    )(x2d, g, wqkvT, wrest, cosT, sinT, sgug, sgub)


def _moba_kernel(qT_ref, kT_ref, vT_ref, oT_ref, kaug_s, vaug_s, ksum_s, *, nb):
    i = pl.program_id(2)
    blk = MOBA_BLOCK

    @pl.when(i == 0)
    def _build_head_state():
        hot_rows = lax.broadcasted_iota(jnp.int32, (HEAD_DIM, blk), 0)
        ones_row = (hot_rows == 0).astype(BF16)
        for j in range(nb):
            kTj = kT_ref[:, j * blk:(j + 1) * blk]
            kaT = jnp.concatenate([kTj, (hot_rows == j).astype(F32)], axis=0)
            ka = kaT.T
            kaug_s[j] = ka.astype(BF16)
            ksum_s[j:j + 1, :] = jnp.sum(ka, axis=0, keepdims=True)
            vaug_s[j] = jnp.concatenate([vT_ref[:, j * blk:(j + 1) * blk], ones_row], axis=0)

    qT = qT_ref[...]
    zeros_f = jnp.zeros((HEAD_DIM, blk), F32)
    kmean = ksum_s[...] * (1.0 / blk)
    gate = jnp.dot(kmean, jnp.concatenate([qT, zeros_f], axis=0),
                   preferred_element_type=F32, precision=lax.Precision.HIGHEST)
    rows = lax.broadcasted_iota(jnp.int32, (nb, blk), 0)
    past = rows < i
    g = jnp.where(past, gate, NEG)
    sel = jnp.zeros((nb, blk), jnp.bool_)
    for _ in range(MOBA_TOPK):
        top = jnp.max(g, axis=0, keepdims=True)
        first = jnp.min(jnp.where(g == top, rows, nb), axis=0, keepdims=True)
        hit = rows == first
        sel = jnp.logical_or(sel, hit)
        g = jnp.where(hit, BELOW_NEG, g)
    sel = jnp.logical_and(sel, past)
    bias = jnp.where(sel, 0.0, NEG).astype(BF16)
    if nb < HEAD_DIM:
        bias = jnp.concatenate([bias, jnp.zeros((HEAD_DIM - nb, blk), BF16)], axis=0)

    q_bf = (qT * (HEAD_DIM ** -0.5)).astype(BF16)
    q_sel = jnp.concatenate([q_bf, bias], axis=0)
    q_own = jnp.concatenate([q_bf, jnp.zeros((HEAD_DIM, blk), BF16)], axis=0)

    def attend(j, carry, q_aug, causal):
        m, acc = carry
        sT = jnp.dot(kaug_s[j], q_aug, preferred_element_type=F32)
        if causal:
            kpos = lax.broadcasted_iota(jnp.int32, (blk, blk), 0)
            qpos = lax.broadcasted_iota(jnp.int32, (blk, blk), 1)
            sT = jnp.where(kpos <= qpos, sT, NEG)
        m_new = jnp.maximum(m, jnp.max(sT, axis=0, keepdims=True))
        alpha = jnp.exp(m - m_new)
        p = jnp.exp(sT - m_new).astype(BF16)
        acc = alpha * acc + jnp.dot(vaug_s[j], p, preferred_element_type=F32)
        return m_new, acc

    carry = (jnp.full((1, blk), BELOW_NEG, F32), jnp.zeros((AUG, blk), F32))
    carry = attend(i, carry, q_own, True)
    _, acc = lax.fori_loop(0, i, lambda j, cr: attend(j, cr, q_sel, False), carry)
    oT_ref[...] = (acc[:HEAD_DIM] / acc[HEAD_DIM:HEAD_DIM + 1]).astype(oT_ref.dtype)


def _moba(qT, kT, vT, batch, seq):
    nb = seq // MOBA_BLOCK
    assert nb <= HEAD_DIM, "one-hot block id must fit beside the head features"
    blk = MOBA_BLOCK
    return pl.pallas_call(
        functools.partial(_moba_kernel, nb=nb),
        out_shape=jax.ShapeDtypeStruct((ATT_DIM, batch * seq), BF16),
        grid=(batch, ATT_HEADS, nb),
        in_specs=[pl.BlockSpec((HEAD_DIM, blk), lambda b, h, i: (h, b * nb + i)),
                  pl.BlockSpec((HEAD_DIM, seq), lambda b, h, i: (h, b)),
                  pl.BlockSpec((HEAD_DIM, seq), lambda b, h, i: (h, b))],
        out_specs=pl.BlockSpec((HEAD_DIM, blk), lambda b, h, i: (h, b * nb + i)),
        scratch_shapes=[pltpu.VMEM((nb, blk, AUG), BF16),
                        pltpu.VMEM((nb, AUG, blk), BF16),
                        pltpu.VMEM((nb, AUG), F32)],
        compiler_params=pltpu.CompilerParams(
            dimension_semantics=("arbitrary", "arbitrary", "arbitrary"),
            vmem_limit_bytes=VMEM_LIMIT),
        name="moba",
    )(qT, kT, vT)


def _mix_kernel(x_ref, xa_ref, xa_halo_ref, zu_ref, zv_ref, hc_ref, hc_halo_ref, ybT_ref,
                g_ref, wpool_ref, pscale_ref, wsgu_ref, bsgu_ref, convw_ref, convb_ref,
                cng_ref, cnb_ref, wgate_ref, bgate_ref, wpa_ref, wpb_ref, wpc_ref, wpd_ref,
                wo_ref, out_ref, xa_ext, hc_ext, *, tiles_per_seq):
    tm = x_ref.shape[0]
    i = pl.program_id(0)
    seq_tile = i % tiles_per_seq
    keep_halo = (seq_tile > 0).astype(F32)

    xa = xa_ref[...]
    xa_ext[0:HALO, :] = xa_halo_ref[...] * keep_halo
    xa_ext[HALO:, :] = xa
    lane = lax.broadcasted_iota(jnp.int32, (tm, POOL_DIM), 1)
    t_seq = seq_tile * tm + lax.broadcasted_iota(jnp.int32, (tm, POOL_DIM), 0)
    shifted = lambda d: xa_ext[HALO - d:HALO - d + tm, :]
    wsum = xa
    win = jnp.zeros((tm, POOL_DIM), F32)
    done = 1
    for gi, w in enumerate(POOL_WINDOWS):
        for d in range(done, w):
            wsum = wsum + shifted(d)
        done = w
        in_group = (lane // POOL_GROUP_DIM) == gi
        cnt = jnp.minimum(t_seq + 1, w).astype(F32)
        win = jnp.where(in_group, wsum / cnt, win)
    pooled = (win - xa).astype(BF16)
    ya = jnp.dot(pooled, wpool_ref[...], preferred_element_type=F32) * pscale_ref[...]

    lane_c = lax.broadcasted_iota(jnp.int32, (SGU_CHUNK, SGU_DIM), 1) // SGU_GROUP_DIM
    yc_parts = []
    for c in range(tm // SGU_CHUNK):
        rs = slice(c * SGU_CHUNK, (c + 1) * SGU_CHUNK)
        f_all = jnp.dot(wsgu_ref[...], zv_ref[rs, :], preferred_element_type=F32)
        f = f_all[0:SGU_CHUNK]
        for gi in range(1, SGU_GROUPS):
            f = jnp.where(lane_c == gi, f_all[gi * SGU_CHUNK:(gi + 1) * SGU_CHUNK], f)
        yc_parts.append(zu_ref[rs, :] * (f + bsgu_ref[...]))
    yc = jnp.concatenate(yc_parts, axis=0)

    hc_ext[0:HALO, :] = hc_halo_ref[...] * keep_halo
    hc_ext[HALO:, :] = hc_ref[...]
    rows = 64
    yd_parts = []
    for r0 in range(0, tm, rows):
        acc = jnp.zeros((rows, CONV_DIM), F32)
        for j in range(CONV_WIDTH):
            start = r0 + HALO - (CONV_WIDTH - 1) + j
            acc = acc + hc_ext[start:start + rows, :] * convw_ref[j:j + 1, :]
        yd_parts.append(acc)
    conv = jnp.concatenate(yd_parts, axis=0) + convb_ref[...]
    yd = _layer_norm(conv, cng_ref[...], cnb_ref[...])
    yd = yd * jax.nn.sigmoid(yd)

    x = x_ref[...]
    xn = _rms(x, g_ref[...]).astype(BF16)
    branch = (
        jnp.dot(ya.astype(BF16), wpa_ref[...], preferred_element_type=F32),
        lax.dot_general(ybT_ref[...], wpb_ref[...], (((0,), (0,)), ((), ())),
                        preferred_element_type=F32),
        jnp.dot(yc.astype(BF16), wpc_ref[...], preferred_element_type=F32),
        jnp.dot(yd.astype(BF16), wpd_ref[...], preferred_element_type=F32),
    )
    merged = jnp.zeros((tm, D_MODEL), F32)
    for b in range(N_BRANCH):
        cols = slice(b * D_MODEL, (b + 1) * D_MODEL)
        logit = jnp.dot(xn, wgate_ref[:, cols], preferred_element_type=F32) + bgate_ref[:, cols]
        merged = merged + jax.nn.sigmoid(logit) * branch[b]
    out_ref[...] = x + jnp.dot(merged.astype(BF16), wo_ref[...], preferred_element_type=F32)


def _mix(x2d, xa, zu, zv, hc, ybT, g, wpool, pscale, wsgu, bsgu, convw, convb, cng, cnb,
         wgate, bgate, wpa, wpb, wpc, wpd, wo, seq):
    n = x2d.shape[0]
    tm = TOKEN_TILE
    halo_blocks = tm // HALO
    tok = lambda w: pl.BlockSpec((tm, w), lambda i: (i, 0))
    halo = lambda w: pl.BlockSpec((HALO, w), lambda i: (jnp.maximum(i * halo_blocks - 1, 0), 0))
    consts = (g, wpool, pscale, wsgu, bsgu, convw, convb, cng, cnb, wgate, bgate,
              wpa, wpb, wpc, wpd, wo)
    return pl.pallas_call(
        functools.partial(_mix_kernel, tiles_per_seq=seq // tm),
        out_shape=jax.ShapeDtypeStruct((n, D_MODEL), F32),
        grid=(n // tm,),
        in_specs=[tok(D_MODEL), tok(POOL_DIM), halo(POOL_DIM), tok(SGU_DIM), tok(SGU_DIM),
                  tok(CONV_DIM), halo(CONV_DIM),
                  pl.BlockSpec((ATT_DIM, tm), lambda i: (0, i))]
                 + [_const_spec(a.shape) for a in consts],
        out_specs=tok(D_MODEL),
        scratch_shapes=[pltpu.VMEM((tm + HALO, POOL_DIM), F32),
                        pltpu.VMEM((tm + HALO, CONV_DIM), F32)],
        compiler_params=pltpu.CompilerParams(dimension_semantics=("parallel",),
                                             vmem_limit_bytes=VMEM_LIMIT),
        name="mix_merge",
    )(x2d, xa, xa, zu, zv, hc, hc, ybT, *consts)


def _ffn_kernel(x_ref, g_ref, wg_ref, wu_ref, wout_ref, gfin_ref, out_ref, *, final_norm):
    x = x_ref[...]
    hn = _rms(x, g_ref[...]).astype(BF16)
    y = x
    for c0 in range(0, FF_DIM, FF_CHUNK):
        cols = slice(c0, c0 + FF_CHUNK)
        gate = jnp.dot(hn, wg_ref[:, cols], preferred_element_type=F32)
        up = jnp.dot(hn, wu_ref[:, cols], preferred_element_type=F32)
        act = (gate * jax.nn.sigmoid(gate) * up).astype(BF16)
        y = y + jnp.dot(act, wout_ref[cols, :], preferred_element_type=F32)
    if final_norm:
        y = _rms(y, gfin_ref[...])
    out_ref[...] = y


def _ffn(x2d, g, wg, wu, wout, gfin, final_norm):
    n = x2d.shape[0]
    tm = TOKEN_TILE
    tok = pl.BlockSpec((tm, D_MODEL), lambda i: (i, 0))
    return pl.pallas_call(
        functools.partial(_ffn_kernel, final_norm=final_norm),
        out_shape=jax.ShapeDtypeStruct((n, D_MODEL), F32),
        grid=(n // tm,),
        in_specs=[tok, _const_spec((1, D_MODEL)), _const_spec(wg.shape), _const_spec(wu.shape),
                  _const_spec(wout.shape), _const_spec((1, D_MODEL))],
        out_specs=tok,
        compiler_params=pltpu.CompilerParams(dimension_semantics=("parallel",),
                                             vmem_limit_bytes=VMEM_LIMIT),
        name="swiglu_ffn",
    )(x2d, g, wg, wu, wout, gfin)


def _block_diag(blocks):
    g, r, c = blocks.shape
    out = jnp.zeros((g * r, g * c), blocks.dtype)
    for i in range(g):
        out = out.at[i * r:(i + 1) * r, i * c:(i + 1) * c].set(blocks[i])
    return out


def kernel(x, positions, norm_mix_g, w_in, w_gate, b_gate, pool_w, pool_scale, sgu_norm_g, sgu_norm_b, sgu_w, sgu_b, conv_w, conv_b, conv_norm_g, conv_norm_b, w_proj_a, w_proj_b, w_proj_c, w_proj_d, w_o, norm_ffn_g, w_ffn_in, w_ffn_out, final_norm_g):
    batch, seq, d = x.shape
    depth = w_in.shape[0]
    n = batch * seq
    assert d == D_MODEL and seq % TOKEN_TILE == 0 and TOKEN_TILE % MOBA_BLOCK == 0
    row = lambda v: v.reshape(1, -1).astype(F32)

    inv_freq = 1.0 / (ROPE_THETA ** (jnp.arange(HALF_DIM, dtype=F32) / HALF_DIM))
    cosT, sinT = _rope_tables(positions.astype(F32).reshape(1, n), inv_freq.reshape(HALF_DIM, 1))

    causal = jnp.tril(jnp.ones((SGU_CHUNK, SGU_CHUNK), dtype=bool))
    qkv_end = POOL_DIM + 3 * ATT_DIM
    x2d = x.reshape(n, d)
    for l in range(depth):
        wqkvT = w_in[l][:, POOL_DIM:qkv_end].T.astype(BF16)
        wrest = jnp.concatenate([w_in[l][:, :POOL_DIM], w_in[l][:, qkv_end:]], axis=1).astype(BF16)
        wsgu = jnp.where(causal[None], sgu_w[l], 0.0).reshape(SGU_GROUPS * SGU_CHUNK, SGU_CHUNK)
        bsgu = jnp.repeat(sgu_b[l].T, SGU_GROUP_DIM, axis=1)

        qT, kT, vT, xa, zu, zv, hc = _inproj(
            x2d, row(norm_mix_g[l]), wqkvT, wrest, cosT, sinT,
            row(sgu_norm_g[l]), row(sgu_norm_b[l]))
        ybT = _moba(qT, kT, vT, batch, seq)
        x2d = _mix(x2d, xa, zu, zv, hc, ybT, row(norm_mix_g[l]),
                   _block_diag(pool_w[l]).astype(BF16), row(pool_scale[l]),
                   wsgu.astype(BF16), bsgu.astype(F32), conv_w[l].astype(F32), row(conv_b[l]),
                   row(conv_norm_g[l]), row(conv_norm_b[l]),
                   w_gate[l].astype(BF16), row(b_gate[l]),
                   w_proj_a[l].astype(BF16), w_proj_b[l].astype(BF16),
                   w_proj_c[l].astype(BF16), w_proj_d[l].astype(BF16),
                   w_o[l].astype(BF16), seq)
        x2d = _ffn(x2d, row(norm_ffn_g[l]), w_ffn_in[l][:, :FF_DIM].astype(BF16),
                   w_ffn_in[l][:, FF_DIM:].astype(BF16), w_ffn_out[l].astype(BF16),
                   row(final_norm_g), final_norm=(l == depth - 1))
    return x2d.reshape(batch, seq, d)
```

```python
import functools
import math

import jax
import jax.numpy as jnp
from jax import lax
from jax.experimental import pallas as pl
from jax.experimental.pallas import tpu as pltpu

F32 = jnp.float32
BF16 = jnp.bfloat16

D_MODEL = 1024
POOL_GROUPS = 4
POOL_GROUP_DIM = 64
POOL_WINDOWS = (2, 4, 8, 16)
POOL_DIM = POOL_GROUPS * POOL_GROUP_DIM
ATT_HEADS = 8
HEAD_DIM = 64
HALF_DIM = HEAD_DIM // 2
ATT_DIM = ATT_HEADS * HEAD_DIM
MOBA_BLOCK = 256
MOBA_TOPK = 3
ROPE_THETA = 10000.0
SGU_GROUPS = 4
SGU_GROUP_DIM = 64
SGU_DIM = SGU_GROUPS * SGU_GROUP_DIM
SGU_CHUNK = 128
CONV_DIM = 256
CONV_WIDTH = 31
N_BRANCH = 4
FF_DIM = 2816
EPS = 1e-6
NEG = -1e30
LOG2_E = 1.4426950408889634
BELOW_NEG = -3e38

HALO = 32
TOKEN_TILE = 512
FF_CHUNK = 1408
AUG = 2 * HEAD_DIM
V_ROWS = HEAD_DIM + 16
MOBA_UNROLL = 4
VMEM_LIMIT = 56 * 1024 * 1024


def _rms(x, g):
    return x * lax.rsqrt(jnp.mean(x * x, axis=-1, keepdims=True) + EPS) * g


def _layer_norm(x, g, b):
    mu = jnp.mean(x, axis=-1, keepdims=True)
    xc = x - mu
    var = jnp.mean(xc * xc, axis=-1, keepdims=True)
    return xc * lax.rsqrt(var + EPS) * g + b


def _const_spec(shape):
    nd = len(shape)
    return pl.BlockSpec(shape, lambda *_: (0,) * nd, pipeline_mode=pl.Buffered(1))


def _rope_table_kernel(pos_ref, invf_ref, cos_ref, sin_ref):
    ang = invf_ref[...] * pos_ref[...]
    cos_ref[...] = jnp.cos(ang)
    sin_ref[...] = jnp.sin(ang)


def _rope_tables(pos_row, invf_col):
    n = pos_row.shape[1]
    tn = 2048
    return pl.pallas_call(
        _rope_table_kernel,
        out_shape=(jax.ShapeDtypeStruct((HALF_DIM, n), F32),) * 2,
        grid=(n // tn,),
        in_specs=[pl.BlockSpec((1, tn), lambda i: (0, i)),
                  pl.BlockSpec((HALF_DIM, 1), lambda i: (0, 0))],
        out_specs=(pl.BlockSpec((HALF_DIM, tn), lambda i: (0, i)),) * 2,
        name="rope_tables",
    )(pos_row, invf_col)


def _inproj_kernel(x_ref, g_ref, wqkvT_ref, wrest_ref, cos_ref, sin_ref, sgug_ref, sgub_ref,
                   qT_ref, kT_ref, vT_ref, xa_ref, zu_ref, zv_ref, hc_ref):
    xn = _rms(x_ref[...], g_ref[...]).astype(BF16)
    pT = lax.dot_general(wqkvT_ref[...], xn, (((1,), (1,)), ((), ())),
                         preferred_element_type=F32)
    c = cos_ref[...]
    s = sin_ref[...]
    for base, out_ref in ((0, qT_ref), (ATT_DIM, kT_ref)):
        for h in range(ATT_HEADS):
            r0 = base + h * HEAD_DIM
            x1 = pT[r0:r0 + HALF_DIM]
            x2 = pT[r0 + HALF_DIM:r0 + HEAD_DIM]
            o0 = h * HEAD_DIM
            out_ref[o0:o0 + HALF_DIM, :] = x1 * c - x2 * s
            out_ref[o0 + HALF_DIM:o0 + HEAD_DIM, :] = x2 * c + x1 * s
    vT_ref[...] = pT[2 * ATT_DIM:].astype(BF16)

    r = jnp.dot(xn, wrest_ref[...], preferred_element_type=F32)
    xa_ref[...] = r[:, :POOL_DIM]
    o = POOL_DIM
    zu_ref[...] = jax.nn.gelu(r[:, o:o + SGU_DIM], approximate=True)
    zv = jax.nn.gelu(r[:, o + SGU_DIM:o + 2 * SGU_DIM], approximate=True)
    zv_ref[...] = _layer_norm(zv, sgug_ref[...], sgub_ref[...]).astype(BF16)
    o += 2 * SGU_DIM
    hc_ref[...] = r[:, o:o + CONV_DIM] * jax.nn.sigmoid(r[:, o + CONV_DIM:o + 2 * CONV_DIM])


def _inproj(x2d, g, wqkvT, wrest, cosT, sinT, sgug, sgub):
    n = x2d.shape[0]
    tm = TOKEN_TILE
    n_rest = wrest.shape[1]
    tok = lambda w: pl.BlockSpec((tm, w), lambda i: (i, 0))
    feat = lambda r: pl.BlockSpec((r, tm), lambda i: (0, i))
    return pl.pallas_call(
        _inproj_kernel,
        out_shape=(jax.ShapeDtypeStruct((ATT_DIM, n), F32),
                   jax.ShapeDtypeStruct((ATT_DIM, n), F32),
                   jax.ShapeDtypeStruct((ATT_DIM, n), BF16),
                   jax.ShapeDtypeStruct((n, POOL_DIM), F32),
                   jax.ShapeDtypeStruct((n, SGU_DIM), F32),
                   jax.ShapeDtypeStruct((n, SGU_DIM), BF16),
                   jax.ShapeDtypeStruct((n, CONV_DIM), F32)),
        grid=(n // tm,),
        in_specs=[tok(D_MODEL), _const_spec((1, D_MODEL)),
                  _const_spec((3 * ATT_DIM, D_MODEL)), _const_spec((D_MODEL, n_rest)),
                  feat(HALF_DIM), feat(HALF_DIM),
                  _const_spec((1, SGU_DIM)), _const_spec((1, SGU_DIM))],
        out_specs=(feat(ATT_DIM), feat(ATT_DIM), feat(ATT_DIM),
                   tok(POOL_DIM), tok(SGU_DIM), tok(SGU_DIM), tok(CONV_DIM)),
        compiler_params=pltpu.CompilerParams(dimension_semantics=("parallel",),
                                             vmem_limit_bytes=VMEM_LIMIT),
        name="inproj",
    )(x2d, g, wqkvT, wrest, cosT, sinT, sgug, sgub)


def _moba_kernel(qT_ref, kT_ref, vT_ref, oT_ref, kaug_s, vaug_s, ksum_s, sa_s, sb_s,
                 *, nb, unroll):
    i = pl.program_id(2)
    blk = MOBA_BLOCK

    @pl.when(i == 0)
    def _build_head_state():
        hot_rows = lax.broadcasted_iota(jnp.int32, (HEAD_DIM, blk), 0)
        ones_row = (lax.broadcasted_iota(jnp.int32, (V_ROWS - HEAD_DIM, blk), 0) == 0).astype(BF16)
        for j in range(nb):
            kTj = kT_ref[:, j * blk:(j + 1) * blk]
            kaT = jnp.concatenate([kTj, (hot_rows == j).astype(F32)], axis=0)
            ka = kaT.T
            kaug_s[j * blk:(j + 1) * blk, :] = ka.astype(BF16)
            ksum_s[j:j + 1, :] = jnp.sum(ka, axis=0, keepdims=True)
            vaug_s[j] = jnp.concatenate([vT_ref[:, j * blk:(j + 1) * blk], ones_row], axis=0)

    qT = qT_ref[...]
    kmean = ksum_s[...] * (1.0 / blk)
    gate = jnp.dot(kmean, jnp.concatenate([qT, jnp.zeros((HEAD_DIM, blk), F32)], axis=0),
                   preferred_element_type=F32, precision=lax.Precision.HIGHEST)
    rows = lax.broadcasted_iota(jnp.int32, (nb, blk), 0)
    past = rows < i
    g = jnp.where(past, gate, NEG)
    sel = jnp.zeros((nb, blk), jnp.bool_)
    for _ in range(MOBA_TOPK):
        top = jnp.max(g, axis=0, keepdims=True)
        first = jnp.min(jnp.where(g == top, rows, nb), axis=0, keepdims=True)
        hit = rows == first
        sel = jnp.logical_or(sel, hit)
        g = jnp.where(hit, BELOW_NEG, g)
    sel = jnp.logical_and(sel, past)
    bias = jnp.where(sel, 0.0, NEG).astype(BF16)
    if nb < HEAD_DIM:
        bias = jnp.concatenate([bias, jnp.zeros((HEAD_DIM - nb, blk), BF16)], axis=0)
    q_bf = (qT * (HEAD_DIM ** -0.5 * LOG2_E)).astype(BF16)
    q_aug = jnp.concatenate([q_bf, bias], axis=0)
    q_own = jnp.concatenate([q_bf, jnp.zeros((HEAD_DIM, blk), BF16)], axis=0)
    grp = unroll * blk

    k_own = kaug_s[pl.ds(pl.multiple_of(i * blk, blk), blk), :]
    s_own = jnp.dot(k_own, q_own, preferred_element_type=F32)
    causal = (lax.broadcasted_iota(jnp.int32, (blk, blk), 0)
              <= lax.broadcasted_iota(jnp.int32, (blk, blk), 1))
    s_own = jnp.where(causal, s_own, NEG)
    m_own = jnp.max(s_own, axis=0, keepdims=True)
    p_own = jnp.exp2(s_own - m_own).astype(BF16)
    carry = (m_own, jnp.dot(vaug_s[i], p_own, preferred_element_type=F32))

    def scores(gi, dst):
        rows_g = kaug_s[pl.ds(pl.multiple_of(gi * grp, grp), grp), :]
        dst[...] = jnp.dot(rows_g, q_aug, preferred_element_type=F32)

    def attend(gi, src, cr):
        maxes, outs = [], []
        for u in range(unroll):
            tile = slice(u * blk, (u + 1) * blk)
            m_u = jnp.max(src[tile, :], axis=0, keepdims=True)
            p = jnp.exp2(src[tile, :] - m_u).astype(BF16)
            outs.append(jnp.dot(vaug_s[gi * unroll + u], p, preferred_element_type=F32))
            maxes.append(m_u)
        m, acc = cr
        m_new = m
        for m_u in maxes:
            m_new = jnp.maximum(m_new, m_u)
        acc = acc * jnp.exp2(m - m_new)
        for m_u, o_u in zip(maxes, outs):
            acc = acc + o_u * jnp.exp2(m_u - m_new)
        return m_new, acc

    n_groups = (i + unroll - 1) // unroll
    last_group = nb // unroll - 1
    scores(0, sa_s)

    def pair(t, cr):
        g0 = 2 * t
        scores(g0 + 1, sb_s)
        cr = attend(g0, sa_s, cr)
        scores(jnp.minimum(g0 + 2, last_group), sa_s)
        return attend(g0 + 1, sb_s, cr)

    _, acc = lax.fori_loop(0, (n_groups + 1) // 2, pair, carry)
    oT_ref[...] = (acc[:HEAD_DIM] / acc[HEAD_DIM:HEAD_DIM + 1]).astype(oT_ref.dtype)


def _moba(qT, kT, vT, batch, seq):
    nb = seq // MOBA_BLOCK
    assert nb <= HEAD_DIM, "one-hot block id must fit beside the head features"
    blk = MOBA_BLOCK
    assert nb % 2 == 0
    unroll = math.gcd(nb // 2, MOBA_UNROLL)
    return pl.pallas_call(
        functools.partial(_moba_kernel, nb=nb, unroll=unroll),
        out_shape=jax.ShapeDtypeStruct((ATT_DIM, batch * seq), BF16),
        grid=(batch, ATT_HEADS, nb),
        in_specs=[pl.BlockSpec((HEAD_DIM, blk), lambda b, h, i: (h, b * nb + i)),
                  pl.BlockSpec((HEAD_DIM, seq), lambda b, h, i: (h, b)),
                  pl.BlockSpec((HEAD_DIM, seq), lambda b, h, i: (h, b))],
        out_specs=pl.BlockSpec((HEAD_DIM, blk), lambda b, h, i: (h, b * nb + i)),
        scratch_shapes=[pltpu.VMEM((nb * blk, AUG), BF16),
                        pltpu.VMEM((nb, V_ROWS, blk), BF16),
                        pltpu.VMEM((nb, AUG), F32),
                        pltpu.VMEM((unroll * blk, blk), F32),
                        pltpu.VMEM((unroll * blk, blk), F32)],
        compiler_params=pltpu.CompilerParams(
            dimension_semantics=("arbitrary", "arbitrary", "arbitrary"),
            vmem_limit_bytes=VMEM_LIMIT),
        name="moba",
    )(qT, kT, vT)


def _mix_kernel(x_ref, xa_ref, xa_halo_ref, zu_ref, zv_ref, hc_ref, hc_halo_ref, ybT_ref,
                g_ref, wpool_ref, pscale_ref, wsgu_ref, bsgu_ref, convw_ref, convb_ref,
                cng_ref, cnb_ref, wgate_ref, bgate_ref, wpa_ref, wpb_ref, wpc_ref, wpd_ref,
                wo_ref, out_ref, xa_ext, hc_ext, *, tiles_per_seq):
    tm = x_ref.shape[0]
    i = pl.program_id(0)
    seq_tile = i % tiles_per_seq
    keep_halo = (seq_tile > 0).astype(F32)

    xa = xa_ref[...]
    xa_ext[0:HALO, :] = xa_halo_ref[...] * keep_halo
    xa_ext[HALO:, :] = xa
    lane = lax.broadcasted_iota(jnp.int32, (tm, POOL_DIM), 1)
    t_seq = seq_tile * tm + lax.broadcasted_iota(jnp.int32, (tm, POOL_DIM), 0)
    shifted = lambda d: xa_ext[HALO - d:HALO - d + tm, :]
    wsum = xa
    win = jnp.zeros((tm, POOL_DIM), F32)
    done = 1
    for gi, w in enumerate(POOL_WINDOWS):
        for d in range(done, w):
            wsum = wsum + shifted(d)
        done = w
        in_group = (lane // POOL_GROUP_DIM) == gi
        cnt = jnp.minimum(t_seq + 1, w).astype(F32)
        win = jnp.where(in_group, wsum / cnt, win)
    pooled = (win - xa).astype(BF16)
    ya = jnp.dot(pooled, wpool_ref[...], preferred_element_type=F32) * pscale_ref[...]

    lane_c = lax.broadcasted_iota(jnp.int32, (SGU_CHUNK, SGU_DIM), 1) // SGU_GROUP_DIM
    yc_parts = []
    for c in range(tm // SGU_CHUNK):
        rs = slice(c * SGU_CHUNK, (c + 1) * SGU_CHUNK)
        f_all = jnp.dot(wsgu_ref[...], zv_ref[rs, :], preferred_element_type=F32)
        f = f_all[0:SGU_CHUNK]
        for gi in range(1, SGU_GROUPS):
            f = jnp.where(lane_c == gi, f_all[gi * SGU_CHUNK:(gi + 1) * SGU_CHUNK], f)
        yc_parts.append(zu_ref[rs, :] * (f + bsgu_ref[...]))
    yc = jnp.concatenate(yc_parts, axis=0)

    hc_ext[0:HALO, :] = hc_halo_ref[...] * keep_halo
    hc_ext[HALO:, :] = hc_ref[...]
    rows = 64
    yd_parts = []
    for r0 in range(0, tm, rows):
        acc = jnp.zeros((rows, CONV_DIM), F32)
        for j in range(CONV_WIDTH):
            start = r0 + HALO - (CONV_WIDTH - 1) + j
            acc = acc + hc_ext[start:start + rows, :] * convw_ref[j:j + 1, :]
        yd_parts.append(acc)
    conv = jnp.concatenate(yd_parts, axis=0) + convb_ref[...]
    yd = _layer_norm(conv, cng_ref[...], cnb_ref[...])
    yd = yd * jax.nn.sigmoid(yd)

    x = x_ref[...]
    xn = _rms(x, g_ref[...]).astype(BF16)
    branch = (
        jnp.dot(ya.astype(BF16), wpa_ref[...], preferred_element_type=F32),
        lax.dot_general(ybT_ref[...], wpb_ref[...], (((0,), (0,)), ((), ())),
                        preferred_element_type=F32),
        jnp.dot(yc.astype(BF16), wpc_ref[...], preferred_element_type=F32),
        jnp.dot(yd.astype(BF16), wpd_ref[...], preferred_element_type=F32),
    )
    merged = jnp.zeros((tm, D_MODEL), F32)
    for b in range(N_BRANCH):
        cols = slice(b * D_MODEL, (b + 1) * D_MODEL)
        logit = jnp.dot(xn, wgate_ref[:, cols], preferred_element_type=F32) + bgate_ref[:, cols]
        merged = merged + jax.nn.sigmoid(logit) * branch[b]
    out_ref[...] = x + jnp.dot(merged.astype(BF16), wo_ref[...], preferred_element_type=F32)


def _mix(x2d, xa, zu, zv, hc, ybT, g, wpool, pscale, wsgu, bsgu, convw, convb, cng, cnb,
         wgate, bgate, wpa, wpb, wpc, wpd, wo, seq):
    n = x2d.shape[0]
    tm = TOKEN_TILE
    halo_blocks = tm // HALO
    tok = lambda w: pl.BlockSpec((tm, w), lambda i: (i, 0))
    halo = lambda w: pl.BlockSpec((HALO, w), lambda i: (jnp.maximum(i * halo_blocks - 1, 0), 0))
    consts = (g, wpool, pscale, wsgu, bsgu, convw, convb, cng, cnb, wgate, bgate,
              wpa, wpb, wpc, wpd, wo)
    return pl.pallas_call(
        functools.partial(_mix_kernel, tiles_per_seq=seq // tm),
        out_shape=jax.ShapeDtypeStruct((n, D_MODEL), F32),
        grid=(n // tm,),
        in_specs=[tok(D_MODEL), tok(POOL_DIM), halo(POOL_DIM), tok(SGU_DIM), tok(SGU_DIM),
                  tok(CONV_DIM), halo(CONV_DIM),
                  pl.BlockSpec((ATT_DIM, tm), lambda i: (0, i))]
                 + [_const_spec(a.shape) for a in consts],
        out_specs=tok(D_MODEL),
        scratch_shapes=[pltpu.VMEM((tm + HALO, POOL_DIM), F32),
                        pltpu.VMEM((tm + HALO, CONV_DIM), F32)],
        compiler_params=pltpu.CompilerParams(dimension_semantics=("parallel",),
                                             vmem_limit_bytes=VMEM_LIMIT),
        name="mix_merge",
    )(x2d, xa, xa, zu, zv, hc, hc, ybT, *consts)


def _ffn_kernel(x_ref, g_ref, wg_ref, wu_ref, wout_ref, gfin_ref, out_ref, *, final_norm):
    x = x_ref[...]
    hn = _rms(x, g_ref[...]).astype(BF16)
    y = x
    for c0 in range(0, FF_DIM, FF_CHUNK):
        cols = slice(c0, c0 + FF_CHUNK)
        gate = jnp.dot(hn, wg_ref[:, cols], preferred_element_type=F32)
        up = jnp.dot(hn, wu_ref[:, cols], preferred_element_type=F32)
        act = (gate * jax.nn.sigmoid(gate) * up).astype(BF16)
        y = y + jnp.dot(act, wout_ref[cols, :], preferred_element_type=F32)
    if final_norm:
        y = _rms(y, gfin_ref[...])
    out_ref[...] = y


def _ffn(x2d, g, wg, wu, wout, gfin, final_norm):
    n = x2d.shape[0]
    tm = TOKEN_TILE
    tok = pl.BlockSpec((tm, D_MODEL), lambda i: (i, 0))
    return pl.pallas_call(
        functools.partial(_ffn_kernel, final_norm=final_norm),
        out_shape=jax.ShapeDtypeStruct((n, D_MODEL), F32),
        grid=(n // tm,),
        in_specs=[tok, _const_spec((1, D_MODEL)), _const_spec(wg.shape), _const_spec(wu.shape),
                  _const_spec(wout.shape), _const_spec((1, D_MODEL))],
        out_specs=tok,
        compiler_params=pltpu.CompilerParams(dimension_semantics=("parallel",),
                                             vmem_limit_bytes=VMEM_LIMIT),
        name="swiglu_ffn",
    )(x2d, g, wg, wu, wout, gfin)


def _block_diag(blocks):
    g, r, c = blocks.shape
    out = jnp.zeros((g * r, g * c), blocks.dtype)
    for i in range(g):
        out = out.at[i * r:(i + 1) * r, i * c:(i + 1) * c].set(blocks[i])
    return out


def kernel(x, positions, norm_mix_g, w_in, w_gate, b_gate, pool_w, pool_scale, sgu_norm_g, sgu_norm_b, sgu_w, sgu_b, conv_w, conv_b, conv_norm_g, conv_norm_b, w_proj_a, w_proj_b, w_proj_c, w_proj_d, w_o, norm_ffn_g, w_ffn_in, w_ffn_out, final_norm_g):
    batch, seq, d = x.shape
    depth = w_in.shape[0]
    n = batch * seq
    assert d == D_MODEL and seq % TOKEN_TILE == 0 and TOKEN_TILE % MOBA_BLOCK == 0
    row = lambda v: v.reshape(1, -1).astype(F32)

    inv_freq = 1.0 / (ROPE_THETA ** (jnp.arange(HALF_DIM, dtype=F32) / HALF_DIM))
    cosT, sinT = _rope_tables(positions.astype(F32).reshape(1, n), inv_freq.reshape(HALF_DIM, 1))

    causal = jnp.tril(jnp.ones((SGU_CHUNK, SGU_CHUNK), dtype=bool))
    qkv_end = POOL_DIM + 3 * ATT_DIM
    x2d = x.reshape(n, d)
    for l in range(depth):
        wqkvT = w_in[l][:, POOL_DIM:qkv_end].T.astype(BF16)
        wrest = jnp.concatenate([w_in[l][:, :POOL_DIM], w_in[l][:, qkv_end:]], axis=1).astype(BF16)
        wsgu = jnp.where(causal[None], sgu_w[l], 0.0).reshape(SGU_GROUPS * SGU_CHUNK, SGU_CHUNK)
        bsgu = jnp.repeat(sgu_b[l].T, SGU_GROUP_DIM, axis=1)

        qT, kT, vT, xa, zu, zv, hc = _inproj(
            x2d, row(norm_mix_g[l]), wqkvT, wrest, cosT, sinT,
            row(sgu_norm_g[l]), row(sgu_norm_b[l]))
        ybT = _moba(qT, kT, vT, batch, seq)
        x2d = _mix(x2d, xa, zu, zv, hc, ybT, row(norm_mix_g[l]),
                   _block_diag(pool_w[l]).astype(BF16), row(pool_scale[l]),
                   wsgu.astype(BF16), bsgu.astype(F32), conv_w[l].astype(F32), row(conv_b[l]),
                   row(conv_norm_g[l]), row(conv_norm_b[l]),
                   w_gate[l].astype(BF16), row(b_gate[l]),
                   w_proj_a[l].astype(BF16), w_proj_b[l].astype(BF16),
                   w_proj_c[l].astype(BF16), w_proj_d[l].astype(BF16),
                   w_o[l].astype(BF16), seq)
        x2d = _ffn(x2d, row(norm_ffn_g[l]), w_ffn_in[l][:, :FF_DIM].astype(BF16),
                   w_ffn_in[l][:, FF_DIM:].astype(BF16), w_ffn_out[l].astype(BF16),
                   row(final_norm_g), final_norm=(l == depth - 1))
    return x2d.reshape(batch, seq, d)
```

```python
import functools
import math

import jax
import jax.numpy as jnp
from jax import lax
from jax.experimental import pallas as pl
from jax.experimental.pallas import tpu as pltpu

F32 = jnp.float32
BF16 = jnp.bfloat16

D_MODEL = 1024
POOL_GROUPS = 4
POOL_GROUP_DIM = 64
POOL_WINDOWS = (2, 4, 8, 16)
POOL_DIM = POOL_GROUPS * POOL_GROUP_DIM
ATT_HEADS = 8
HEAD_DIM = 64
HALF_DIM = HEAD_DIM // 2
ATT_DIM = ATT_HEADS * HEAD_DIM
MOBA_BLOCK = 256
MOBA_TOPK = 3
ROPE_THETA = 10000.0
SGU_GROUPS = 4
SGU_GROUP_DIM = 64
SGU_DIM = SGU_GROUPS * SGU_GROUP_DIM
SGU_CHUNK = 128
CONV_DIM = 256
CONV_WIDTH = 31
N_BRANCH = 4
FF_DIM = 2816
EPS = 1e-6
NEG = -1e30
LOG2_E = 1.4426950408889634
BELOW_NEG = -3e38

HALO = 32
TOKEN_TILE = 512
FF_CHUNK = 1408
AUG = 2 * HEAD_DIM
V_ROWS = HEAD_DIM + 16
MOBA_UNROLL = 2
VMEM_LIMIT = 56 * 1024 * 1024


def _rms(x, g):
    return x * lax.rsqrt(jnp.mean(x * x, axis=-1, keepdims=True) + EPS) * g


def _layer_norm(x, g, b):
    mu = jnp.mean(x, axis=-1, keepdims=True)
    xc = x - mu
    var = jnp.mean(xc * xc, axis=-1, keepdims=True)
    return xc * lax.rsqrt(var + EPS) * g + b


def _const_spec(shape):
    nd = len(shape)
    return pl.BlockSpec(shape, lambda *_: (0,) * nd, pipeline_mode=pl.Buffered(1))


def _rope_table_kernel(pos_ref, invf_ref, cos_ref, sin_ref):
    ang = invf_ref[...] * pos_ref[...]
    cos_ref[...] = jnp.cos(ang)
    sin_ref[...] = jnp.sin(ang)


def _rope_tables(pos_row, invf_col):
    n = pos_row.shape[1]
    tn = 2048
    return pl.pallas_call(
        _rope_table_kernel,
        out_shape=(jax.ShapeDtypeStruct((HALF_DIM, n), F32),) * 2,
        grid=(n // tn,),
        in_specs=[pl.BlockSpec((1, tn), lambda i: (0, i)),
                  pl.BlockSpec((HALF_DIM, 1), lambda i: (0, 0))],
        out_specs=(pl.BlockSpec((HALF_DIM, tn), lambda i: (0, i)),) * 2,
        name="rope_tables",
    )(pos_row, invf_col)


def _inproj_kernel(x_ref, g_ref, wqkvT_ref, wrest_ref, cos_ref, sin_ref, sgug_ref, sgub_ref,
                   qT_ref, kT_ref, vT_ref, xa_ref, zu_ref, zv_ref, hc_ref):
    xn = _rms(x_ref[...], g_ref[...]).astype(BF16)
    pT = lax.dot_general(wqkvT_ref[...], xn, (((1,), (1,)), ((), ())),
                         preferred_element_type=F32)
    c = cos_ref[...]
    s = sin_ref[...]
    for base, out_ref in ((0, qT_ref), (ATT_DIM, kT_ref)):
        for h in range(ATT_HEADS):
            r0 = base + h * HEAD_DIM
            x1 = pT[r0:r0 + HALF_DIM]
            x2 = pT[r0 + HALF_DIM:r0 + HEAD_DIM]
            o0 = h * HEAD_DIM
            out_ref[o0:o0 + HALF_DIM, :] = x1 * c - x2 * s
            out_ref[o0 + HALF_DIM:o0 + HEAD_DIM, :] = x2 * c + x1 * s
    vT_ref[...] = pT[2 * ATT_DIM:].astype(BF16)

    r = jnp.dot(xn, wrest_ref[...], preferred_element_type=F32)
    xa_ref[...] = r[:, :POOL_DIM]
    o = POOL_DIM
    zu_ref[...] = jax.nn.gelu(r[:, o:o + SGU_DIM], approximate=True)
    zv = jax.nn.gelu(r[:, o + SGU_DIM:o + 2 * SGU_DIM], approximate=True)
    zv_ref[...] = _layer_norm(zv, sgug_ref[...], sgub_ref[...]).astype(BF16)
    o += 2 * SGU_DIM
    hc_ref[...] = r[:, o:o + CONV_DIM] * jax.nn.sigmoid(r[:, o + CONV_DIM:o + 2 * CONV_DIM])


def _inproj(x2d, g, wqkvT, wrest, cosT, sinT, sgug, sgub):
    n = x2d.shape[0]
    tm = TOKEN_TILE
    n_rest = wrest.shape[1]
    tok = lambda w: pl.BlockSpec((tm, w), lambda i: (i, 0))
    feat = lambda r: pl.BlockSpec((r, tm), lambda i: (0, i))
    return pl.pallas_call(
        _inproj_kernel,
        out_shape=(jax.ShapeDtypeStruct((ATT_DIM, n), F32),
                   jax.ShapeDtypeStruct((ATT_DIM, n), F32),
                   jax.ShapeDtypeStruct((ATT_DIM, n), BF16),
                   jax.ShapeDtypeStruct((n, POOL_DIM), F32),
                   jax.ShapeDtypeStruct((n, SGU_DIM), F32),
                   jax.ShapeDtypeStruct((n, SGU_DIM), BF16),
                   jax.ShapeDtypeStruct((n, CONV_DIM), F32)),
        grid=(n // tm,),
        in_specs=[tok(D_MODEL), _const_spec((1, D_MODEL)),
                  _const_spec((3 * ATT_DIM, D_MODEL)), _const_spec((D_MODEL, n_rest)),
                  feat(HALF_DIM), feat(HALF_DIM),
                  _const_spec((1, SGU_DIM)), _const_spec((1, SGU_DIM))],
        out_specs=(feat(ATT_DIM), feat(ATT_DIM), feat(ATT_DIM),
                   tok(POOL_DIM), tok(SGU_DIM), tok(SGU_DIM), tok(CONV_DIM)),
        compiler_params=pltpu.CompilerParams(dimension_semantics=("parallel",),
                                             vmem_limit_bytes=VMEM_LIMIT),
        name="inproj",
    )(x2d, g, wqkvT, wrest, cosT, sinT, sgug, sgub)


def _moba_kernel(qT_ref, kT_ref, vT_ref, oT_ref, kaug_s, vaug_s, ksum_s, *stage, nb, unroll):
    i = pl.program_id(2)
    blk = MOBA_BLOCK

    @pl.when(i == 0)
    def _build_head_state():
        hot_rows = lax.broadcasted_iota(jnp.int32, (HEAD_DIM, blk), 0)
        ones_row = (lax.broadcasted_iota(jnp.int32, (V_ROWS - HEAD_DIM, blk), 0) == 0).astype(BF16)
        for j in range(nb):
            kTj = kT_ref[:, j * blk:(j + 1) * blk]
            kaT = jnp.concatenate([kTj, (hot_rows == j).astype(F32)], axis=0)
            ka = kaT.T
            kaug_s[j * blk:(j + 1) * blk, :] = ka.astype(BF16)
            ksum_s[j:j + 1, :] = jnp.sum(ka, axis=0, keepdims=True)
            vaug_s[j] = jnp.concatenate([vT_ref[:, j * blk:(j + 1) * blk], ones_row], axis=0)

    qT = qT_ref[...]
    kmean = ksum_s[...] * (1.0 / blk)
    gate = jnp.dot(kmean, jnp.concatenate([qT, jnp.zeros((HEAD_DIM, blk), F32)], axis=0),
                   preferred_element_type=F32, precision=lax.Precision.HIGHEST)
    rows = lax.broadcasted_iota(jnp.int32, (nb, blk), 0)
    past = rows < i
    g = jnp.where(past, gate, NEG)
    sel = jnp.zeros((nb, blk), jnp.bool_)
    for _ in range(MOBA_TOPK):
        top = jnp.max(g, axis=0, keepdims=True)
        first = jnp.min(jnp.where(g == top, rows, nb), axis=0, keepdims=True)
        hit = rows == first
        sel = jnp.logical_or(sel, hit)
        g = jnp.where(hit, BELOW_NEG, g)
    sel = jnp.logical_and(sel, past)
    bias_f = jnp.where(sel, 0.0, NEG)
    bias = bias_f.astype(BF16)
    if nb < HEAD_DIM:
        bias = jnp.concatenate([bias, jnp.zeros((HEAD_DIM - nb, blk), BF16)], axis=0)
    q_bf = (qT * (HEAD_DIM ** -0.5 * LOG2_E)).astype(BF16)
    q_aug = jnp.concatenate([q_bf, bias], axis=0)
    q_own = jnp.concatenate([q_bf, jnp.zeros((HEAD_DIM, blk), BF16)], axis=0)
    grp = unroll * blk

    k_own = kaug_s[pl.ds(pl.multiple_of(i * blk, blk), blk), :]
    s_own = jnp.dot(k_own, q_own, preferred_element_type=F32)
    causal = (lax.broadcasted_iota(jnp.int32, (blk, blk), 0)
              <= lax.broadcasted_iota(jnp.int32, (blk, blk), 1))
    s_own = jnp.where(causal, s_own, NEG)
    m_own = jnp.max(s_own, axis=0, keepdims=True)
    p_own = jnp.exp2(s_own - m_own).astype(BF16)
    carry = (m_own, jnp.dot(vaug_s[i], p_own, preferred_element_type=F32))

    def scores(gi, dst):
        rows_g = kaug_s[pl.ds(pl.multiple_of(gi * grp, grp), grp), :]
        dst[...] = jnp.dot(rows_g, q_aug, preferred_element_type=F32)

    def attend(gi, src, cr):
        maxes, outs = [], []
        for u in range(unroll):
            tile = slice(u * blk, (u + 1) * blk)
            m_u = jnp.max(src[tile, :], axis=0, keepdims=True)
            p = jnp.exp2(src[tile, :] - m_u).astype(BF16)
            outs.append(jnp.dot(vaug_s[gi * unroll + u], p, preferred_element_type=F32))
            maxes.append(m_u)
        m, acc = cr
        m_new = m
        for m_u in maxes:
            m_new = jnp.maximum(m_new, m_u)
        acc = acc * jnp.exp2(m - m_new)
        for m_u, o_u in zip(maxes, outs):
            acc = acc + o_u * jnp.exp2(m_u - m_new)
        return m_new, acc

    n_groups = (i + unroll - 1) // unroll
    last_group = nb // unroll - 1
    raw = jnp.dot(kaug_s[0:2 * grp, :], q_own, preferred_element_type=F32)
    for b in range(2 * unroll):
        k, u = divmod(b, unroll)
        stage[k][u * blk:(u + 1) * blk, :] = raw[b * blk:(b + 1) * blk] + bias_f[b:b + 1, :]

    def trip(t, cr):
        g0 = 4 * t
        for half in range(2):
            cur, nxt = (stage[:2], stage[2:]) if half == 0 else (stage[2:], stage[:2])
            base = g0 + 2 * half
            for k in range(2):
                scores(jnp.minimum(base + 2 + k, last_group), nxt[k])
            for k in range(2):
                cr = attend(base + k, cur[k], cr)
        return cr

    _, acc = lax.fori_loop(0, (n_groups + 3) // 4, trip, carry)
    oT_ref[...] = (acc[:HEAD_DIM] / acc[HEAD_DIM:HEAD_DIM + 1]).astype(oT_ref.dtype)


def _moba(qT, kT, vT, batch, seq):
    nb = seq // MOBA_BLOCK
    assert nb <= HEAD_DIM, "one-hot block id must fit beside the head features"
    blk = MOBA_BLOCK
    assert nb % 4 == 0
    unroll = math.gcd(nb // 4, MOBA_UNROLL)
    return pl.pallas_call(
        functools.partial(_moba_kernel, nb=nb, unroll=unroll),
        out_shape=jax.ShapeDtypeStruct((ATT_DIM, batch * seq), BF16),
        grid=(batch, ATT_HEADS, nb),
        in_specs=[pl.BlockSpec((HEAD_DIM, blk), lambda b, h, i: (h, b * nb + i)),
                  pl.BlockSpec((HEAD_DIM, seq), lambda b, h, i: (h, b)),
                  pl.BlockSpec((HEAD_DIM, seq), lambda b, h, i: (h, b))],
        out_specs=pl.BlockSpec((HEAD_DIM, blk), lambda b, h, i: (h, b * nb + i)),
        scratch_shapes=[pltpu.VMEM((nb * blk, AUG), BF16),
                        pltpu.VMEM((nb, V_ROWS, blk), BF16),
                        pltpu.VMEM((nb, AUG), F32)]
                       + [pltpu.VMEM((unroll * blk, blk), F32)] * 4,
        compiler_params=pltpu.CompilerParams(
            dimension_semantics=("arbitrary", "arbitrary", "arbitrary"),
            vmem_limit_bytes=VMEM_LIMIT),
        name="moba",
    )(qT, kT, vT)


def _mix_kernel(x_ref, xa_ref, xa_halo_ref, zu_ref, zv_ref, hc_ref, hc_halo_ref, ybT_ref,
                g_ref, wpool_ref, pscale_ref, wsgu_ref, bsgu_ref, convw_ref, convb_ref,
                cng_ref, cnb_ref, wgate_ref, bgate_ref, wpa_ref, wpb_ref, wpc_ref, wpd_ref,
                wo_ref, out_ref, xa_ext, hc_ext, *, tiles_per_seq):
    tm = x_ref.shape[0]
    i = pl.program_id(0)
    seq_tile = i % tiles_per_seq
    keep_halo = (seq_tile > 0).astype(F32)

    xa = xa_ref[...]
    xa_ext[0:HALO, :] = xa_halo_ref[...] * keep_halo
    xa_ext[HALO:, :] = xa
    lane = lax.broadcasted_iota(jnp.int32, (tm, POOL_DIM), 1)
    t_seq = seq_tile * tm + lax.broadcasted_iota(jnp.int32, (tm, POOL_DIM), 0)
    shifted = lambda d: xa_ext[HALO - d:HALO - d + tm, :]
    wsum = xa
    win = jnp.zeros((tm, POOL_DIM), F32)
    done = 1
    for gi, w in enumerate(POOL_WINDOWS):
        for d in range(done, w):
            wsum = wsum + shifted(d)
        done = w
        in_group = (lane // POOL_GROUP_DIM) == gi
        cnt = jnp.minimum(t_seq + 1, w).astype(F32)
        win = jnp.where(in_group, wsum / cnt, win)
    pooled = (win - xa).astype(BF16)
    ya = jnp.dot(pooled, wpool_ref[...], preferred_element_type=F32) * pscale_ref[...]

    lane_c = lax.broadcasted_iota(jnp.int32, (SGU_CHUNK, SGU_DIM), 1) // SGU_GROUP_DIM
    yc_parts = []
    for c in range(tm // SGU_CHUNK):
        rs = slice(c * SGU_CHUNK, (c + 1) * SGU_CHUNK)
        f_all = jnp.dot(wsgu_ref[...], zv_ref[rs, :], preferred_element_type=F32)
        f = f_all[0:SGU_CHUNK]
        for gi in range(1, SGU_GROUPS):
            f = jnp.where(lane_c == gi, f_all[gi * SGU_CHUNK:(gi + 1) * SGU_CHUNK], f)
        yc_parts.append(zu_ref[rs, :] * (f + bsgu_ref[...]))
    yc = jnp.concatenate(yc_parts, axis=0)

    hc_ext[0:HALO, :] = hc_halo_ref[...] * keep_halo
    hc_ext[HALO:, :] = hc_ref[...]
    rows = 64
    yd_parts = []
    for r0 in range(0, tm, rows):
        acc = jnp.zeros((rows, CONV_DIM), F32)
        for j in range(CONV_WIDTH):
            start = r0 + HALO - (CONV_WIDTH - 1) + j
            acc = acc + hc_ext[start:start + rows, :] * convw_ref[j:j + 1, :]
        yd_parts.append(acc)
    conv = jnp.concatenate(yd_parts, axis=0) + convb_ref[...]
    yd = _layer_norm(conv, cng_ref[...], cnb_ref[...])
    yd = yd * jax.nn.sigmoid(yd)

    x = x_ref[...]
    xn = _rms(x, g_ref[...]).astype(BF16)
    branch = (
        jnp.dot(ya.astype(BF16), wpa_ref[...], preferred_element_type=F32),
        lax.dot_general(ybT_ref[...], wpb_ref[...], (((0,), (0,)), ((), ())),
                        preferred_element_type=F32),
        jnp.dot(yc.astype(BF16), wpc_ref[...], preferred_element_type=F32),
        jnp.dot(yd.astype(BF16), wpd_ref[...], preferred_element_type=F32),
    )
    merged = jnp.zeros((tm, D_MODEL), F32)
    for b in range(N_BRANCH):
        cols = slice(b * D_MODEL, (b + 1) * D_MODEL)
        logit = jnp.dot(xn, wgate_ref[:, cols], preferred_element_type=F32) + bgate_ref[:, cols]
        merged = merged + jax.nn.sigmoid(logit) * branch[b]
    out_ref[...] = x + jnp.dot(merged.astype(BF16), wo_ref[...], preferred_element_type=F32)


def _mix(x2d, xa, zu, zv, hc, ybT, g, wpool, pscale, wsgu, bsgu, convw, convb, cng, cnb,
         wgate, bgate, wpa, wpb, wpc, wpd, wo, seq):
    n = x2d.shape[0]
    tm = TOKEN_TILE
    halo_blocks = tm // HALO
    tok = lambda w: pl.BlockSpec((tm, w), lambda i: (i, 0))
    halo = lambda w: pl.BlockSpec((HALO, w), lambda i: (jnp.maximum(i * halo_blocks - 1, 0), 0))
    consts = (g, wpool, pscale, wsgu, bsgu, convw, convb, cng, cnb, wgate, bgate,
              wpa, wpb, wpc, wpd, wo)
    return pl.pallas_call(
        functools.partial(_mix_kernel, tiles_per_seq=seq // tm),
        out_shape=jax.ShapeDtypeStruct((n, D_MODEL), F32),
        grid=(n // tm,),
        in_specs=[tok(D_MODEL), tok(POOL_DIM), halo(POOL_DIM), tok(SGU_DIM), tok(SGU_DIM),
                  tok(CONV_DIM), halo(CONV_DIM),
                  pl.BlockSpec((ATT_DIM, tm), lambda i: (0, i))]
                 + [_const_spec(a.shape) for a in consts],
        out_specs=tok(D_MODEL),
        scratch_shapes=[pltpu.VMEM((tm + HALO, POOL_DIM), F32),
                        pltpu.VMEM((tm + HALO, CONV_DIM), F32)],
        compiler_params=pltpu.CompilerParams(dimension_semantics=("parallel",),
                                             vmem_limit_bytes=VMEM_LIMIT),
        name="mix_merge",
    )(x2d, xa, xa, zu, zv, hc, hc, ybT, *consts)


def _ffn_kernel(x_ref, g_ref, wg_ref, wu_ref, wout_ref, gfin_ref, out_ref, *, final_norm):
    x = x_ref[...]
    hn = _rms(x, g_ref[...]).astype(BF16)
    y = x
    for c0 in range(0, FF_DIM, FF_CHUNK):
        cols = slice(c0, c0 + FF_CHUNK)
        gate = jnp.dot(hn, wg_ref[:, cols], preferred_element_type=F32)
        up = jnp.dot(hn, wu_ref[:, cols], preferred_element_type=F32)
        act = (gate * jax.nn.sigmoid(gate) * up).astype(BF16)
        y = y + jnp.dot(act, wout_ref[cols, :], preferred_element_type=F32)
    if final_norm:
        y = _rms(y, gfin_ref[...])
    out_ref[...] = y


def _ffn(x2d, g, wg, wu, wout, gfin, final_norm):
    n = x2d.shape[0]
    tm = TOKEN_TILE
    tok = pl.BlockSpec((tm, D_MODEL), lambda i: (i, 0))
    return pl.pallas_call(
        functools.partial(_ffn_kernel, final_norm=final_norm),
        out_shape=jax.ShapeDtypeStruct((n, D_MODEL), F32),
        grid=(n // tm,),
        in_specs=[tok, _const_spec((1, D_MODEL)), _const_spec(wg.shape), _const_spec(wu.shape),
                  _const_spec(wout.shape), _const_spec((1, D_MODEL))],
        out_specs=tok,
        compiler_params=pltpu.CompilerParams(dimension_semantics=("parallel",),
                                             vmem_limit_bytes=VMEM_LIMIT),
        name="swiglu_ffn",
    )(x2d, g, wg, wu, wout, gfin)


def _block_diag(blocks):
    g, r, c = blocks.shape
    out = jnp.zeros((g * r, g * c), blocks.dtype)
    for i in range(g):
        out = out.at[i * r:(i + 1) * r, i * c:(i + 1) * c].set(blocks[i])
    return out


def kernel(x, positions, norm_mix_g, w_in, w_gate, b_gate, pool_w, pool_scale, sgu_norm_g, sgu_norm_b, sgu_w, sgu_b, conv_w, conv_b, conv_norm_g, conv_norm_b, w_proj_a, w_proj_b, w_proj_c, w_proj_d, w_o, norm_ffn_g, w_ffn_in, w_ffn_out, final_norm_g):
    batch, seq, d = x.shape
    depth = w_in.shape[0]
    n = batch * seq
    assert d == D_MODEL and seq % TOKEN_TILE == 0 and TOKEN_TILE % MOBA_BLOCK == 0
    row = lambda v: v.reshape(1, -1).astype(F32)

    inv_freq = 1.0 / (ROPE_THETA ** (jnp.arange(HALF_DIM, dtype=F32) / HALF_DIM))
    cosT, sinT = _rope_tables(positions.astype(F32).reshape(1, n), inv_freq.reshape(HALF_DIM, 1))

    causal = jnp.tril(jnp.ones((SGU_CHUNK, SGU_CHUNK), dtype=bool))
    qkv_end = POOL_DIM + 3 * ATT_DIM
    x2d = x.reshape(n, d)
    for l in range(depth):
        wqkvT = w_in[l][:, POOL_DIM:qkv_end].T.astype(BF16)
        wrest = jnp.concatenate([w_in[l][:, :POOL_DIM], w_in[l][:, qkv_end:]], axis=1).astype(BF16)
        wsgu = jnp.where(causal[None], sgu_w[l], 0.0).reshape(SGU_GROUPS * SGU_CHUNK, SGU_CHUNK)
        bsgu = jnp.repeat(sgu_b[l].T, SGU_GROUP_DIM, axis=1)

        qT, kT, vT, xa, zu, zv, hc = _inproj(
            x2d, row(norm_mix_g[l]), wqkvT, wrest, cosT, sinT,
            row(sgu_norm_g[l]), row(sgu_norm_b[l]))
        ybT = _moba(qT, kT, vT, batch, seq)
        x2d = _mix(x2d, xa, zu, zv, hc, ybT, row(norm_mix_g[l]),
                   _block_diag(pool_w[l]).astype(BF16), row(pool_scale[l]),
                   wsgu.astype(BF16), bsgu.astype(F32), conv_w[l].astype(F32), row(conv_b[l]),
                   row(conv_norm_g[l]), row(conv_norm_b[l]),
                   w_gate[l].astype(BF16), row(b_gate[l]),
                   w_proj_a[l].astype(BF16), w_proj_b[l].astype(BF16),
                   w_proj_c[l].astype(BF16), w_proj_d[l].astype(BF16),
                   w_o[l].astype(BF16), seq)
        x2d = _ffn(x2d, row(norm_ffn_g[l]), w_ffn_in[l][:, :FF_DIM].astype(BF16),
                   w_ffn_in[l][:, FF_DIM:].astype(BF16), w_ffn_out[l].astype(BF16),
                   row(final_norm_g), final_norm=(l == depth - 1))
    return x2d.reshape(batch, seq, d)
```

```python
import functools
import math

import jax
import jax.numpy as jnp
from jax import lax
from jax.experimental import pallas as pl
from jax.experimental.pallas import tpu as pltpu

F32 = jnp.float32
BF16 = jnp.bfloat16

D_MODEL = 1024
POOL_GROUPS = 4
POOL_GROUP_DIM = 64
POOL_WINDOWS = (2, 4, 8, 16)
POOL_DIM = POOL_GROUPS * POOL_GROUP_DIM
ATT_HEADS = 8
HEAD_DIM = 64
HALF_DIM = HEAD_DIM // 2
ATT_DIM = ATT_HEADS * HEAD_DIM
MOBA_BLOCK = 256
MOBA_TOPK = 3
ROPE_THETA = 10000.0
SGU_GROUPS = 4
SGU_GROUP_DIM = 64
SGU_DIM = SGU_GROUPS * SGU_GROUP_DIM
SGU_CHUNK = 128
CONV_DIM = 256
CONV_WIDTH = 31
N_BRANCH = 4
FF_DIM = 2816
EPS = 1e-6
NEG = -1e30
LOG2_E = 1.4426950408889634
BELOW_NEG = -3e38

HALO = 32
TOKEN_TILE = 512
FF_CHUNK = 1408
AUG = 2 * HEAD_DIM
V_ROWS = HEAD_DIM + 16
MOBA_UNROLL = 2
VMEM_LIMIT = 56 * 1024 * 1024


def _rms(x, g):
    return x * lax.rsqrt(jnp.mean(x * x, axis=-1, keepdims=True) + EPS) * g


def _layer_norm(x, g, b):
    mu = jnp.mean(x, axis=-1, keepdims=True)
    xc = x - mu
    var = jnp.mean(xc * xc, axis=-1, keepdims=True)
    return xc * lax.rsqrt(var + EPS) * g + b


def _const_spec(shape):
    nd = len(shape)
    return pl.BlockSpec(shape, lambda *_: (0,) * nd, pipeline_mode=pl.Buffered(1))


def _rope_table_kernel(pos_ref, invf_ref, cos_ref, sin_ref):
    ang = invf_ref[...] * pos_ref[...]
    cos_ref[...] = jnp.cos(ang)
    sin_ref[...] = jnp.sin(ang)


def _rope_tables(pos_row, invf_col):
    n = pos_row.shape[1]
    tn = 2048
    return pl.pallas_call(
        _rope_table_kernel,
        out_shape=(jax.ShapeDtypeStruct((HALF_DIM, n), F32),) * 2,
        grid=(n // tn,),
        in_specs=[pl.BlockSpec((1, tn), lambda i: (0, i)),
                  pl.BlockSpec((HALF_DIM, 1), lambda i: (0, 0))],
        out_specs=(pl.BlockSpec((HALF_DIM, tn), lambda i: (0, i)),) * 2,
        name="rope_tables",
    )(pos_row, invf_col)


def _inproj_kernel(x_ref, g_ref, wqkvT_ref, wrest_ref, cos_ref, sin_ref, sgug_ref, sgub_ref,
                   qT_ref, kT_ref, vT_ref, xa_ref, zu_ref, zv_ref, hc_ref):
    xn = _rms(x_ref[...], g_ref[...]).astype(BF16)
    pT = lax.dot_general(wqkvT_ref[...], xn, (((1,), (1,)), ((), ())),
                         preferred_element_type=F32)
    c = cos_ref[...]
    s = sin_ref[...]
    for base, out_ref in ((0, qT_ref), (ATT_DIM, kT_ref)):
        for h in range(ATT_HEADS):
            r0 = base + h * HEAD_DIM
            x1 = pT[r0:r0 + HALF_DIM]
            x2 = pT[r0 + HALF_DIM:r0 + HEAD_DIM]
            o0 = h * HEAD_DIM
            out_ref[o0:o0 + HALF_DIM, :] = x1 * c - x2 * s
            out_ref[o0 + HALF_DIM:o0 + HEAD_DIM, :] = x2 * c + x1 * s
    vT_ref[...] = pT[2 * ATT_DIM:].astype(BF16)

    r = jnp.dot(xn, wrest_ref[...], preferred_element_type=F32)
    xa_ref[...] = r[:, :POOL_DIM]
    o = POOL_DIM
    zu_ref[...] = jax.nn.gelu(r[:, o:o + SGU_DIM], approximate=True)
    zv = jax.nn.gelu(r[:, o + SGU_DIM:o + 2 * SGU_DIM], approximate=True)
    zv_ref[...] = _layer_norm(zv, sgug_ref[...], sgub_ref[...]).astype(BF16)
    o += 2 * SGU_DIM
    hc_ref[...] = r[:, o:o + CONV_DIM] * jax.nn.sigmoid(r[:, o + CONV_DIM:o + 2 * CONV_DIM])


def _inproj(x2d, g, wqkvT, wrest, cosT, sinT, sgug, sgub):
    n = x2d.shape[0]
    tm = TOKEN_TILE
    n_rest = wrest.shape[1]
    tok = lambda w: pl.BlockSpec((tm, w), lambda i: (i, 0))
    feat = lambda r: pl.BlockSpec((r, tm), lambda i: (0, i))
    return pl.pallas_call(
        _inproj_kernel,
        out_shape=(jax.ShapeDtypeStruct((ATT_DIM, n), F32),
                   jax.ShapeDtypeStruct((ATT_DIM, n), F32),
                   jax.ShapeDtypeStruct((ATT_DIM, n), BF16),
                   jax.ShapeDtypeStruct((n, POOL_DIM), F32),
                   jax.ShapeDtypeStruct((n, SGU_DIM), F32),
                   jax.ShapeDtypeStruct((n, SGU_DIM), BF16),
                   jax.ShapeDtypeStruct((n, CONV_DIM), F32)),
        grid=(n // tm,),
        in_specs=[tok(D_MODEL), _const_spec((1, D_MODEL)),
                  _const_spec((3 * ATT_DIM, D_MODEL)), _const_spec((D_MODEL, n_rest)),
                  feat(HALF_DIM), feat(HALF_DIM),
                  _const_spec((1, SGU_DIM)), _const_spec((1, SGU_DIM))],
        out_specs=(feat(ATT_DIM), feat(ATT_DIM), feat(ATT_DIM),
                   tok(POOL_DIM), tok(SGU_DIM), tok(SGU_DIM), tok(CONV_DIM)),
        compiler_params=pltpu.CompilerParams(dimension_semantics=("parallel",),
                                             vmem_limit_bytes=VMEM_LIMIT),
        name="inproj",
    )(x2d, g, wqkvT, wrest, cosT, sinT, sgug, sgub)


def _moba_kernel(qT_ref, kT_ref, vT_ref, oT_ref, kaug_s, vaug_s, ksum_s, *stage, nb, unroll):
    blk = MOBA_BLOCK
    hot_rows = lax.broadcasted_iota(jnp.int32, (HEAD_DIM, blk), 0)
    ones_row = (lax.broadcasted_iota(jnp.int32, (V_ROWS - HEAD_DIM, blk), 0) == 0).astype(BF16)
    for j in range(nb):
        kTj = kT_ref[:, j * blk:(j + 1) * blk]
        kaT = jnp.concatenate([kTj, (hot_rows == j).astype(F32)], axis=0)
        ka = kaT.T
        kaug_s[j * blk:(j + 1) * blk, :] = ka.astype(BF16)
        ksum_s[j:j + 1, :] = jnp.sum(ka, axis=0, keepdims=True)
        vaug_s[j] = jnp.concatenate([vT_ref[:, j * blk:(j + 1) * blk], ones_row], axis=0)

    def query_block(i, _):
        _moba_query_block(i, qT_ref, oT_ref, kaug_s, vaug_s, ksum_s, stage, nb, unroll)
        return 0

    lax.fori_loop(0, nb, query_block, 0)


def _moba_query_block(i, qT_ref, oT_ref, kaug_s, vaug_s, ksum_s, stage, nb, unroll):
    blk = MOBA_BLOCK
    q_cols = pl.ds(pl.multiple_of(i * blk, blk), blk)
    qT = qT_ref[:, q_cols]
    kmean = ksum_s[...] * (1.0 / blk)
    gate = jnp.dot(kmean, jnp.concatenate([qT, jnp.zeros((HEAD_DIM, blk), F32)], axis=0),
                   preferred_element_type=F32, precision=lax.Precision.HIGHEST)
    rows = lax.broadcasted_iota(jnp.int32, (nb, blk), 0)
    past = rows < i
    g = jnp.where(past, gate, NEG)
    sel = jnp.zeros((nb, blk), jnp.bool_)
    for _ in range(MOBA_TOPK):
        top = jnp.max(g, axis=0, keepdims=True)
        first = jnp.min(jnp.where(g == top, rows, nb), axis=0, keepdims=True)
        hit = rows == first
        sel = jnp.logical_or(sel, hit)
        g = jnp.where(hit, BELOW_NEG, g)
    sel = jnp.logical_and(sel, past)
    bias_f = jnp.where(sel, 0.0, NEG)
    bias = bias_f.astype(BF16)
    if nb < HEAD_DIM:
        bias = jnp.concatenate([bias, jnp.zeros((HEAD_DIM - nb, blk), BF16)], axis=0)
    q_bf = (qT * (HEAD_DIM ** -0.5 * LOG2_E)).astype(BF16)
    q_aug = jnp.concatenate([q_bf, bias], axis=0)
    q_own = jnp.concatenate([q_bf, jnp.zeros((HEAD_DIM, blk), BF16)], axis=0)
    grp = unroll * blk

    k_own = kaug_s[pl.ds(pl.multiple_of(i * blk, blk), blk), :]
    s_own = jnp.dot(k_own, q_own, preferred_element_type=F32)
    causal = (lax.broadcasted_iota(jnp.int32, (blk, blk), 0)
              <= lax.broadcasted_iota(jnp.int32, (blk, blk), 1))
    s_own = jnp.where(causal, s_own, NEG)
    m_own = jnp.max(s_own, axis=0, keepdims=True)
    p_own = jnp.exp2(s_own - m_own).astype(BF16)
    carry = (m_own, jnp.dot(vaug_s[i], p_own, preferred_element_type=F32))

    def scores(gi, dst):
        rows_g = kaug_s[pl.ds(pl.multiple_of(gi * grp, grp), grp), :]
        dst[...] = jnp.dot(rows_g, q_aug, preferred_element_type=F32)

    def attend(gi, src, cr):
        maxes, outs = [], []
        for u in range(unroll):
            tile = slice(u * blk, (u + 1) * blk)
            m_u = jnp.max(src[tile, :], axis=0, keepdims=True)
            p = jnp.exp2(src[tile, :] - m_u).astype(BF16)
            outs.append(jnp.dot(vaug_s[gi * unroll + u], p, preferred_element_type=F32))
            maxes.append(m_u)
        m, acc = cr
        m_new = m
        for m_u in maxes:
            m_new = jnp.maximum(m_new, m_u)
        acc = acc * jnp.exp2(m - m_new)
        for m_u, o_u in zip(maxes, outs):
            acc = acc + o_u * jnp.exp2(m_u - m_new)
        return m_new, acc

    n_groups = (i + unroll - 1) // unroll
    last_group = nb // unroll - 1
    raw = jnp.dot(kaug_s[0:2 * grp, :], q_own, preferred_element_type=F32)
    for b in range(2 * unroll):
        k, u = divmod(b, unroll)
        stage[k][u * blk:(u + 1) * blk, :] = raw[b * blk:(b + 1) * blk] + bias_f[b:b + 1, :]

    def trip(t, cr):
        g0 = 4 * t
        for half in range(2):
            cur, nxt = (stage[:2], stage[2:]) if half == 0 else (stage[2:], stage[:2])
            base = g0 + 2 * half
            for k in range(2):
                scores(jnp.minimum(base + 2 + k, last_group), nxt[k])
            for k in range(2):
                cr = attend(base + k, cur[k], cr)
        return cr

    _, acc = lax.fori_loop(0, (n_groups + 3) // 4, trip, carry)
    oT_ref[:, q_cols] = (acc[:HEAD_DIM] / acc[HEAD_DIM:HEAD_DIM + 1]).astype(oT_ref.dtype)


def _moba(qT, kT, vT, batch, seq):
    nb = seq // MOBA_BLOCK
    assert nb <= HEAD_DIM, "one-hot block id must fit beside the head features"
    blk = MOBA_BLOCK
    assert nb % 4 == 0
    unroll = math.gcd(nb // 4, MOBA_UNROLL)
    return pl.pallas_call(
        functools.partial(_moba_kernel, nb=nb, unroll=unroll),
        out_shape=jax.ShapeDtypeStruct((ATT_DIM, batch * seq), BF16),
        grid=(batch, ATT_HEADS),
        in_specs=[pl.BlockSpec((HEAD_DIM, seq), lambda b, h: (h, b))] * 3,
        out_specs=pl.BlockSpec((HEAD_DIM, seq), lambda b, h: (h, b)),
        scratch_shapes=[pltpu.VMEM((nb * blk, AUG), BF16),
                        pltpu.VMEM((nb, V_ROWS, blk), BF16),
                        pltpu.VMEM((nb, AUG), F32)]
                       + [pltpu.VMEM((unroll * blk, blk), F32)] * 4,
        compiler_params=pltpu.CompilerParams(
            dimension_semantics=("arbitrary", "arbitrary"),
            vmem_limit_bytes=VMEM_LIMIT),
        name="moba",
    )(qT, kT, vT)


def _mix_kernel(x_ref, xa_ref, xa_halo_ref, zu_ref, zv_ref, hc_ref, hc_halo_ref, ybT_ref,
                g_ref, wpool_ref, pscale_ref, wsgu_ref, bsgu_ref, convw_ref, convb_ref,
                cng_ref, cnb_ref, wgate_ref, bgate_ref, wpa_ref, wpb_ref, wpc_ref, wpd_ref,
                wo_ref, out_ref, xa_ext, hc_ext, *, tiles_per_seq):
    tm = x_ref.shape[0]
    i = pl.program_id(0)
    seq_tile = i % tiles_per_seq
    keep_halo = (seq_tile > 0).astype(F32)

    xa = xa_ref[...]
    xa_ext[0:HALO, :] = xa_halo_ref[...] * keep_halo
    xa_ext[HALO:, :] = xa
    lane = lax.broadcasted_iota(jnp.int32, (tm, POOL_DIM), 1)
    t_seq = seq_tile * tm + lax.broadcasted_iota(jnp.int32, (tm, POOL_DIM), 0)
    shifted = lambda d: xa_ext[HALO - d:HALO - d + tm, :]
    wsum = xa
    win = jnp.zeros((tm, POOL_DIM), F32)
    done = 1
    for gi, w in enumerate(POOL_WINDOWS):
        for d in range(done, w):
            wsum = wsum + shifted(d)
        done = w
        in_group = (lane // POOL_GROUP_DIM) == gi
        cnt = jnp.minimum(t_seq + 1, w).astype(F32)
        win = jnp.where(in_group, wsum / cnt, win)
    pooled = (win - xa).astype(BF16)
    ya = jnp.dot(pooled, wpool_ref[...], preferred_element_type=F32) * pscale_ref[...]

    lane_c = lax.broadcasted_iota(jnp.int32, (SGU_CHUNK, SGU_DIM), 1) // SGU_GROUP_DIM
    yc_parts = []
    for c in range(tm // SGU_CHUNK):
        rs = slice(c * SGU_CHUNK, (c + 1) * SGU_CHUNK)
        f_all = jnp.dot(wsgu_ref[...], zv_ref[rs, :], preferred_element_type=F32)
        f = f_all[0:SGU_CHUNK]
        for gi in range(1, SGU_GROUPS):
            f = jnp.where(lane_c == gi, f_all[gi * SGU_CHUNK:(gi + 1) * SGU_CHUNK], f)
        yc_parts.append(zu_ref[rs, :] * (f + bsgu_ref[...]))
    yc = jnp.concatenate(yc_parts, axis=0)

    hc_ext[0:HALO, :] = hc_halo_ref[...] * keep_halo
    hc_ext[HALO:, :] = hc_ref[...]
    rows = 64
    yd_parts = []
    for r0 in range(0, tm, rows):
        acc = jnp.zeros((rows, CONV_DIM), F32)
        for j in range(CONV_WIDTH):
            start = r0 + HALO - (CONV_WIDTH - 1) + j
            acc = acc + hc_ext[start:start + rows, :] * convw_ref[j:j + 1, :]
        yd_parts.append(acc)
    conv = jnp.concatenate(yd_parts, axis=0) + convb_ref[...]
    yd = _layer_norm(conv, cng_ref[...], cnb_ref[...])
    yd = yd * jax.nn.sigmoid(yd)

    x = x_ref[...]
    xn = _rms(x, g_ref[...]).astype(BF16)
    branch = (
        jnp.dot(ya.astype(BF16), wpa_ref[...], preferred_element_type=F32),
        lax.dot_general(ybT_ref[...], wpb_ref[...], (((0,), (0,)), ((), ())),
                        preferred_element_type=F32),
        jnp.dot(yc.astype(BF16), wpc_ref[...], preferred_element_type=F32),
        jnp.dot(yd.astype(BF16), wpd_ref[...], preferred_element_type=F32),
    )
    merged = jnp.zeros((tm, D_MODEL), F32)
    for b in range(N_BRANCH):
        cols = slice(b * D_MODEL, (b + 1) * D_MODEL)
        logit = jnp.dot(xn, wgate_ref[:, cols], preferred_element_type=F32) + bgate_ref[:, cols]
        merged = merged + jax.nn.sigmoid(logit) * branch[b]
    out_ref[...] = x + jnp.dot(merged.astype(BF16), wo_ref[...], preferred_element_type=F32)


def _mix(x2d, xa, zu, zv, hc, ybT, g, wpool, pscale, wsgu, bsgu, convw, convb, cng, cnb,
         wgate, bgate, wpa, wpb, wpc, wpd, wo, seq):
    n = x2d.shape[0]
    tm = TOKEN_TILE
    halo_blocks = tm // HALO
    tok = lambda w: pl.BlockSpec((tm, w), lambda i: (i, 0))
    halo = lambda w: pl.BlockSpec((HALO, w), lambda i: (jnp.maximum(i * halo_blocks - 1, 0), 0))
    consts = (g, wpool, pscale, wsgu, bsgu, convw, convb, cng, cnb, wgate, bgate,
              wpa, wpb, wpc, wpd, wo)
    return pl.pallas_call(
        functools.partial(_mix_kernel, tiles_per_seq=seq // tm),
        out_shape=jax.ShapeDtypeStruct((n, D_MODEL), F32),
        grid=(n // tm,),
        in_specs=[tok(D_MODEL), tok(POOL_DIM), halo(POOL_DIM), tok(SGU_DIM), tok(SGU_DIM),
                  tok(CONV_DIM), halo(CONV_DIM),
                  pl.BlockSpec((ATT_DIM, tm), lambda i: (0, i))]
                 + [_const_spec(a.shape) for a in consts],
        out_specs=tok(D_MODEL),
        scratch_shapes=[pltpu.VMEM((tm + HALO, POOL_DIM), F32),
                        pltpu.VMEM((tm + HALO, CONV_DIM), F32)],
        compiler_params=pltpu.CompilerParams(dimension_semantics=("parallel",),
                                             vmem_limit_bytes=VMEM_LIMIT),
        name="mix_merge",
    )(x2d, xa, xa, zu, zv, hc, hc, ybT, *consts)


def _ffn_kernel(x_ref, g_ref, wg_ref, wu_ref, wout_ref, gfin_ref, out_ref, *, final_norm):
    x = x_ref[...]
    hn = _rms(x, g_ref[...]).astype(BF16)
    y = x
    for c0 in range(0, FF_DIM, FF_CHUNK):
        cols = slice(c0, c0 + FF_CHUNK)
        gate = jnp.dot(hn, wg_ref[:, cols], preferred_element_type=F32)
        up = jnp.dot(hn, wu_ref[:, cols], preferred_element_type=F32)
        act = (gate * jax.nn.sigmoid(gate) * up).astype(BF16)
        y = y + jnp.dot(act, wout_ref[cols, :], preferred_element_type=F32)
    if final_norm:
        y = _rms(y, gfin_ref[...])
    out_ref[...] = y


def _ffn(x2d, g, wg, wu, wout, gfin, final_norm):
    n = x2d.shape[0]
    tm = TOKEN_TILE
    tok = pl.BlockSpec((tm, D_MODEL), lambda i: (i, 0))
    return pl.pallas_call(
        functools.partial(_ffn_kernel, final_norm=final_norm),
        out_shape=jax.ShapeDtypeStruct((n, D_MODEL), F32),
        grid=(n // tm,),
        in_specs=[tok, _const_spec((1, D_MODEL)), _const_spec(wg.shape), _const_spec(wu.shape),
                  _const_spec(wout.shape), _const_spec((1, D_MODEL))],
        out_specs=tok,
        compiler_params=pltpu.CompilerParams(dimension_semantics=("parallel",),
                                             vmem_limit_bytes=VMEM_LIMIT),
        name="swiglu_ffn",
    )(x2d, g, wg, wu, wout, gfin)


def _block_diag(blocks):
    g, r, c = blocks.shape
    out = jnp.zeros((g * r, g * c), blocks.dtype)
    for i in range(g):
        out = out.at[i * r:(i + 1) * r, i * c:(i + 1) * c].set(blocks[i])
    return out


def kernel(x, positions, norm_mix_g, w_in, w_gate, b_gate, pool_w, pool_scale, sgu_norm_g, sgu_norm_b, sgu_w, sgu_b, conv_w, conv_b, conv_norm_g, conv_norm_b, w_proj_a, w_proj_b, w_proj_c, w_proj_d, w_o, norm_ffn_g, w_ffn_in, w_ffn_out, final_norm_g):
    batch, seq, d = x.shape
    depth = w_in.shape[0]
    n = batch * seq
    assert d == D_MODEL and seq % TOKEN_TILE == 0 and TOKEN_TILE % MOBA_BLOCK == 0
    row = lambda v: v.reshape(1, -1).astype(F32)

    inv_freq = 1.0 / (ROPE_THETA ** (jnp.arange(HALF_DIM, dtype=F32) / HALF_DIM))
    cosT, sinT = _rope_tables(positions.astype(F32).reshape(1, n), inv_freq.reshape(HALF_DIM, 1))

    causal = jnp.tril(jnp.ones((SGU_CHUNK, SGU_CHUNK), dtype=bool))
    qkv_end = POOL_DIM + 3 * ATT_DIM
    x2d = x.reshape(n, d)
    for l in range(depth):
        wqkvT = w_in[l][:, POOL_DIM:qkv_end].T.astype(BF16)
        wrest = jnp.concatenate([w_in[l][:, :POOL_DIM], w_in[l][:, qkv_end:]], axis=1).astype(BF16)
        wsgu = jnp.where(causal[None], sgu_w[l], 0.0).reshape(SGU_GROUPS * SGU_CHUNK, SGU_CHUNK)
        bsgu = jnp.repeat(sgu_b[l].T, SGU_GROUP_DIM, axis=1)

        qT, kT, vT, xa, zu, zv, hc = _inproj(
            x2d, row(norm_mix_g[l]), wqkvT, wrest, cosT, sinT,
            row(sgu_norm_g[l]), row(sgu_norm_b[l]))
        ybT = _moba(qT, kT, vT, batch, seq)
        x2d = _mix(x2d, xa, zu, zv, hc, ybT, row(norm_mix_g[l]),
                   _block_diag(pool_w[l]).astype(BF16), row(pool_scale[l]),
                   wsgu.astype(BF16), bsgu.astype(F32), conv_w[l].astype(F32), row(conv_b[l]),
                   row(conv_norm_g[l]), row(conv_norm_b[l]),
                   w_gate[l].astype(BF16), row(b_gate[l]),
                   w_proj_a[l].astype(BF16), w_proj_b[l].astype(BF16),
                   w_proj_c[l].astype(BF16), w_proj_d[l].astype(BF16),
                   w_o[l].astype(BF16), seq)
        x2d = _ffn(x2d, row(norm_ffn_g[l]), w_ffn_in[l][:, :FF_DIM].astype(BF16),
                   w_ffn_in[l][:, FF_DIM:].astype(BF16), w_ffn_out[l].astype(BF16),
                   row(final_norm_g), final_norm=(l == depth - 1))
    return x2d.reshape(batch, seq, d)
```

```python
import functools
import math

import jax
import jax.numpy as jnp
from jax import lax
from jax.experimental import pallas as pl
from jax.experimental.pallas import tpu as pltpu

F32 = jnp.float32
BF16 = jnp.bfloat16

D_MODEL = 1024
POOL_GROUPS = 4
POOL_GROUP_DIM = 64
POOL_WINDOWS = (2, 4, 8, 16)
POOL_DIM = POOL_GROUPS * POOL_GROUP_DIM
ATT_HEADS = 8
HEAD_DIM = 64
HALF_DIM = HEAD_DIM // 2
ATT_DIM = ATT_HEADS * HEAD_DIM
MOBA_BLOCK = 256
MOBA_TOPK = 3
ROPE_THETA = 10000.0
SGU_GROUPS = 4
SGU_GROUP_DIM = 64
SGU_DIM = SGU_GROUPS * SGU_GROUP_DIM
SGU_CHUNK = 128
CONV_DIM = 256
CONV_WIDTH = 31
N_BRANCH = 4
FF_DIM = 2816
EPS = 1e-6
NEG = -1e30
LOG2_E = 1.4426950408889634
BELOW_NEG = -3e38

SUBLANES = 8
HALO = 32
TOKEN_TILE = 512
FF_CHUNK = 1408
AUG = 2 * HEAD_DIM
V_ROWS = HEAD_DIM + 16
MOBA_UNROLL = 2
TRIP_HALVES = (4, 2)
VMEM_LIMIT = 56 * 1024 * 1024


def _rms(x, g):
    return x * lax.rsqrt(jnp.mean(x * x, axis=-1, keepdims=True) + EPS) * g


def _layer_norm(x, g, b):
    mu = jnp.mean(x, axis=-1, keepdims=True)
    xc = x - mu
    var = jnp.mean(xc * xc, axis=-1, keepdims=True)
    return xc * lax.rsqrt(var + EPS) * g + b


def _const_spec(shape):
    nd = len(shape)
    return pl.BlockSpec(shape, lambda *_: (0,) * nd, pipeline_mode=pl.Buffered(1))


def _rope_table_kernel(pos_ref, invf_ref, cos_ref, sin_ref):
    ang = invf_ref[...] * pos_ref[...]
    cos_ref[...] = jnp.cos(ang)
    sin_ref[...] = jnp.sin(ang)


def _rope_tables(pos_row, invf_col):
    n = pos_row.shape[1]
    tn = 2048
    return pl.pallas_call(
        _rope_table_kernel,
        out_shape=(jax.ShapeDtypeStruct((HALF_DIM, n), F32),) * 2,
        grid=(n // tn,),
        in_specs=[pl.BlockSpec((1, tn), lambda i: (0, i)),
                  pl.BlockSpec((HALF_DIM, 1), lambda i: (0, 0))],
        out_specs=(pl.BlockSpec((HALF_DIM, tn), lambda i: (0, i)),) * 2,
        name="rope_tables",
    )(pos_row, invf_col)


def _inproj_kernel(x_ref, g_ref, wqkvT_ref, wrest_ref, cos_ref, sin_ref, sgug_ref, sgub_ref,
                   qT_ref, kT_ref, vT_ref, xa_ref, zu_ref, zv_ref, hc_ref):
    xn = _rms(x_ref[...], g_ref[...]).astype(BF16)
    pT = lax.dot_general(wqkvT_ref[...], xn, (((1,), (1,)), ((), ())),
                         preferred_element_type=F32)
    c = cos_ref[...]
    s = sin_ref[...]
    for base, out_ref in ((0, qT_ref), (ATT_DIM, kT_ref)):
        for h in range(ATT_HEADS):
            r0 = base + h * HEAD_DIM
            x1 = pT[r0:r0 + HALF_DIM]
            x2 = pT[r0 + HALF_DIM:r0 + HEAD_DIM]
            o0 = h * HEAD_DIM
            out_ref[o0:o0 + HALF_DIM, :] = x1 * c - x2 * s
            out_ref[o0 + HALF_DIM:o0 + HEAD_DIM, :] = x2 * c + x1 * s
    vT_ref[...] = pT[2 * ATT_DIM:].astype(BF16)

    r = jnp.dot(xn, wrest_ref[...], preferred_element_type=F32)
    xa_ref[...] = r[:, :POOL_DIM]
    o = POOL_DIM
    zu_ref[...] = jax.nn.gelu(r[:, o:o + SGU_DIM], approximate=True)
    zv = jax.nn.gelu(r[:, o + SGU_DIM:o + 2 * SGU_DIM], approximate=True)
    zv_ref[...] = _layer_norm(zv, sgug_ref[...], sgub_ref[...]).astype(BF16)
    o += 2 * SGU_DIM
    hc_ref[...] = r[:, o:o + CONV_DIM] * jax.nn.sigmoid(r[:, o + CONV_DIM:o + 2 * CONV_DIM])


def _inproj(x2d, g, wqkvT, wrest, cosT, sinT, sgug, sgub):
    n = x2d.shape[0]
    tm = TOKEN_TILE
    n_rest = wrest.shape[1]
    tok = lambda w: pl.BlockSpec((tm, w), lambda i: (i, 0))
    feat = lambda r: pl.BlockSpec((r, tm), lambda i: (0, i))
    return pl.pallas_call(
        _inproj_kernel,
        out_shape=(jax.ShapeDtypeStruct((ATT_DIM, n), F32),
                   jax.ShapeDtypeStruct((ATT_DIM, n), F32),
                   jax.ShapeDtypeStruct((ATT_DIM, n), BF16),
                   jax.ShapeDtypeStruct((n, POOL_DIM), F32),
                   jax.ShapeDtypeStruct((n, SGU_DIM), F32),
                   jax.ShapeDtypeStruct((n, SGU_DIM), BF16),
                   jax.ShapeDtypeStruct((n, CONV_DIM), F32)),
        grid=(n // tm,),
        in_specs=[tok(D_MODEL), _const_spec((1, D_MODEL)),
                  _const_spec((3 * ATT_DIM, D_MODEL)), _const_spec((D_MODEL, n_rest)),
                  feat(HALF_DIM), feat(HALF_DIM),
                  _const_spec((1, SGU_DIM)), _const_spec((1, SGU_DIM))],
        out_specs=(feat(ATT_DIM), feat(ATT_DIM), feat(ATT_DIM),
                   tok(POOL_DIM), tok(SGU_DIM), tok(SGU_DIM), tok(CONV_DIM)),
        compiler_params=pltpu.CompilerParams(dimension_semantics=("parallel",),
                                             vmem_limit_bytes=VMEM_LIMIT),
        name="inproj",
    )(x2d, g, wqkvT, wrest, cosT, sinT, sgug, sgub)


def _moba_kernel(qT_ref, kT_ref, vT_ref, oT_ref, kaug_s, vaug_s, ksum_s, *stage, nb, unroll):
    blk = MOBA_BLOCK
    hot_rows = lax.broadcasted_iota(jnp.int32, (HEAD_DIM, blk), 0)
    ones_row = (lax.broadcasted_iota(jnp.int32, (V_ROWS - HEAD_DIM, blk), 0) == 0).astype(BF16)
    for j in range(nb):
        kTj = kT_ref[:, j * blk:(j + 1) * blk]
        kaT = jnp.concatenate([kTj, (hot_rows == j).astype(F32)], axis=0)
        ka = kaT.T
        kaug_s[j * blk:(j + 1) * blk, :] = ka.astype(BF16)
        ksum_s[j:j + 1, :] = jnp.sum(ka, axis=0, keepdims=True)
        vaug_s[j] = jnp.concatenate([vT_ref[:, j * blk:(j + 1) * blk], ones_row], axis=0)

    def query_block(i, _):
        _moba_query_block(i, qT_ref, oT_ref, kaug_s, vaug_s, ksum_s, stage, nb, unroll)
        return 0

    lax.fori_loop(0, nb, query_block, 0)


def _moba_query_block(i, qT_ref, oT_ref, kaug_s, vaug_s, ksum_s, stage, nb, unroll):
    blk = MOBA_BLOCK
    q_cols = pl.ds(pl.multiple_of(i * blk, blk), blk)
    qT = qT_ref[:, q_cols]
    kmean = ksum_s[...] * (1.0 / blk)
    gate = jnp.dot(kmean, jnp.concatenate([qT, jnp.zeros((HEAD_DIM, blk), F32)], axis=0),
                   preferred_element_type=F32, precision=lax.Precision.HIGHEST)
    rows = lax.broadcasted_iota(jnp.int32, (nb, blk), 0)
    past = rows < i
    g = jnp.where(past, gate, NEG)
    sel = jnp.zeros((nb, blk), jnp.bool_)
    for _ in range(MOBA_TOPK):
        top = jnp.max(g, axis=0, keepdims=True)
        first = jnp.min(jnp.where(g == top, rows, nb), axis=0, keepdims=True)
        hit = rows == first
        sel = jnp.logical_or(sel, hit)
        g = jnp.where(hit, BELOW_NEG, g)
    sel = jnp.logical_and(sel, past)
    bias_f = jnp.where(sel, 0.0, NEG)
    bias = bias_f.astype(BF16)
    if nb < HEAD_DIM:
        bias = jnp.concatenate([bias, jnp.zeros((HEAD_DIM - nb, blk), BF16)], axis=0)
    q_bf = (qT * (HEAD_DIM ** -0.5 * LOG2_E)).astype(BF16)
    q_aug = jnp.concatenate([q_bf, bias], axis=0)
    q_own = jnp.concatenate([q_bf, jnp.zeros((HEAD_DIM, blk), BF16)], axis=0)
    grp = unroll * blk

    k_own = kaug_s[pl.ds(pl.multiple_of(i * blk, blk), blk), :]
    s_own = jnp.dot(k_own, q_own, preferred_element_type=F32)
    causal = (lax.broadcasted_iota(jnp.int32, (blk, blk), 0)
              <= lax.broadcasted_iota(jnp.int32, (blk, blk), 1))
    s_own = jnp.where(causal, s_own, NEG)
    m_own = jnp.max(s_own, axis=0, keepdims=True)
    p_own = jnp.exp2(s_own - m_own).astype(BF16)
    carry = (m_own, jnp.dot(vaug_s[i], p_own, preferred_element_type=F32))

    def scores(gi, dst):
        rows_g = kaug_s[pl.ds(pl.multiple_of(gi * grp, grp), grp), :]
        dst[...] = jnp.dot(rows_g, q_aug, preferred_element_type=F32)

    def attend(gi, src, cr):
        maxes, outs = [], []
        for u in range(unroll):
            tile = slice(u * blk, (u + 1) * blk)
            m_u = jnp.max(src[tile, :], axis=0, keepdims=True)
            p = jnp.exp2(src[tile, :] - m_u).astype(BF16)
            outs.append(jnp.dot(vaug_s[gi * unroll + u], p, preferred_element_type=F32))
            maxes.append(m_u)
        m, acc = cr
        m_new = m
        for m_u in maxes:
            m_new = jnp.maximum(m_new, m_u)
        acc = acc * jnp.exp2(m - m_new)
        for m_u, o_u in zip(maxes, outs):
            acc = acc + o_u * jnp.exp2(m_u - m_new)
        return m_new, acc

    n_groups = (i + unroll - 1) // unroll
    last_group = nb // unroll - 1
    raw = jnp.dot(kaug_s[0:2 * grp, :], q_own, preferred_element_type=F32)
    for b in range(2 * unroll):
        k, u = divmod(b, unroll)
        stage[k][u * blk:(u + 1) * blk, :] = raw[b * blk:(b + 1) * blk] + bias_f[b:b + 1, :]

    def trip(g0, halves, cr):
        for half in range(halves):
            cur, nxt = (stage[:2], stage[2:]) if half % 2 == 0 else (stage[2:], stage[:2])
            base = g0 + 2 * half
            for k in range(2):
                scores(jnp.minimum(base + 2 + k, last_group), nxt[k])
            for k in range(2):
                cr = attend(base + k, cur[k], cr)
        return cr

    done = 0
    for size_idx, halves in enumerate(TRIP_HALVES):
        groups = 2 * halves
        if size_idx + 1 < len(TRIP_HALVES):
            n_trips = (n_groups - done) // groups
        else:
            n_trips = (n_groups - done + groups - 1) // groups
        carry = lax.fori_loop(
            0, n_trips,
            lambda t, cr, done=done, groups=groups, halves=halves: trip(done + groups * t, halves, cr),
            carry)
        done = done + groups * n_trips
    acc = carry[1]
    oT_ref[:, q_cols] = (acc[:HEAD_DIM] / acc[HEAD_DIM:HEAD_DIM + 1]).astype(oT_ref.dtype)


def _moba(qT, kT, vT, batch, seq):
    nb = seq // MOBA_BLOCK
    assert nb <= HEAD_DIM, "one-hot block id must fit beside the head features"
    blk = MOBA_BLOCK
    assert nb % 4 == 0
    unroll = math.gcd(nb // 4, MOBA_UNROLL)
    return pl.pallas_call(
        functools.partial(_moba_kernel, nb=nb, unroll=unroll),
        out_shape=jax.ShapeDtypeStruct((ATT_DIM, batch * seq), BF16),
        grid=(batch, ATT_HEADS),
        in_specs=[pl.BlockSpec((HEAD_DIM, seq), lambda b, h: (h, b))] * 3,
        out_specs=pl.BlockSpec((HEAD_DIM, seq), lambda b, h: (h, b)),
        scratch_shapes=[pltpu.VMEM((nb * blk, AUG), BF16),
                        pltpu.VMEM((nb, V_ROWS, blk), BF16),
                        pltpu.VMEM((nb, AUG), F32)]
                       + [pltpu.VMEM((unroll * blk, blk), F32)] * 4,
        compiler_params=pltpu.CompilerParams(
            dimension_semantics=("arbitrary", "arbitrary"),
            vmem_limit_bytes=VMEM_LIMIT),
        name="moba",
    )(qT, kT, vT)


def _mix_kernel(x_ref, xa_ref, xa_halo_ref, zu_ref, zv_ref, hc_ref, hc_halo_ref, ybT_ref,
                g_ref, wpool_ref, pscale_ref, wsgu_ref, bsgu_ref, convw_ref, convb_ref,
                cng_ref, cnb_ref, wgate_ref, bgate_ref, wpa_ref, wpb_ref, wpc_ref, wpd_ref,
                wo_ref, out_ref, xa_ext, hc_ext, *, tiles_per_seq):
    tm = x_ref.shape[0]
    i = pl.program_id(0)
    seq_tile = i % tiles_per_seq
    keep_halo = (seq_tile > 0).astype(F32)

    xa = xa_ref[...]
    xa_ext[0:HALO, :] = xa_halo_ref[...] * keep_halo
    xa_ext[HALO:, :] = xa
    lane = lax.broadcasted_iota(jnp.int32, (tm, POOL_DIM), 1)
    t_seq = seq_tile * tm + lax.broadcasted_iota(jnp.int32, (tm, POOL_DIM), 0)
    shifted = lambda d: xa_ext[HALO - d:HALO - d + tm, :]
    wsum = xa
    win = jnp.zeros((tm, POOL_DIM), F32)
    done = 1
    for gi, w in enumerate(POOL_WINDOWS):
        for d in range(done, w):
            wsum = wsum + shifted(d)
        done = w
        in_group = (lane // POOL_GROUP_DIM) == gi
        cnt = jnp.minimum(t_seq + 1, w).astype(F32)
        win = jnp.where(in_group, wsum / cnt, win)
    pooled = (win - xa).astype(BF16)
    ya = jnp.dot(pooled, wpool_ref[...], preferred_element_type=F32) * pscale_ref[...]

    lane_c = lax.broadcasted_iota(jnp.int32, (SGU_CHUNK, SGU_DIM), 1) // SGU_GROUP_DIM
    yc_parts = []
    for c in range(tm // SGU_CHUNK):
        rs = slice(c * SGU_CHUNK, (c + 1) * SGU_CHUNK)
        f_all = jnp.dot(wsgu_ref[...], zv_ref[rs, :], preferred_element_type=F32)
        f = f_all[0:SGU_CHUNK]
        for gi in range(1, SGU_GROUPS):
            f = jnp.where(lane_c == gi, f_all[gi * SGU_CHUNK:(gi + 1) * SGU_CHUNK], f)
        yc_parts.append(zu_ref[rs, :] * (f + bsgu_ref[...]))
    yc = jnp.concatenate(yc_parts, axis=0)

    hc_ext[0:HALO, :] = hc_halo_ref[...] * keep_halo
    hc_ext[HALO:, :] = hc_ref[...]
    rows = 64
    first_off = HALO - (CONV_WIDTH - 1)
    yd_parts = []
    for r0 in range(0, tm, rows):
        acc = None
        for shift in range(SUBLANES):
            slab = None
            for off in range(shift, first_off + CONV_WIDTH, SUBLANES):
                j = off - first_off
                if j < 0:
                    continue
                base = r0 + off - shift
                if shift == 0:
                    term = hc_ext[base:base + rows, :] * convw_ref[j:j + 1, :]
                else:
                    term = hc_ext[base:base + rows + SUBLANES, :] * convw_ref[j:j + 1, :]
                slab = term if slab is None else slab + term
            part = slab if shift == 0 else slab[shift:shift + rows]
            acc = part if acc is None else acc + part
        yd_parts.append(acc)
    conv = jnp.concatenate(yd_parts, axis=0) + convb_ref[...]
    yd = _layer_norm(conv, cng_ref[...], cnb_ref[...])
    yd = yd * jax.nn.sigmoid(yd)

    x = x_ref[...]
    xn = _rms(x, g_ref[...]).astype(BF16)
    branch = (
        jnp.dot(ya.astype(BF16), wpa_ref[...], preferred_element_type=F32),
        lax.dot_general(ybT_ref[...], wpb_ref[...], (((0,), (0,)), ((), ())),
                        preferred_element_type=F32),
        jnp.dot(yc.astype(BF16), wpc_ref[...], preferred_element_type=F32),
        jnp.dot(yd.astype(BF16), wpd_ref[...], preferred_element_type=F32),
    )
    merged = jnp.zeros((tm, D_MODEL), F32)
    for b in range(N_BRANCH):
        cols = slice(b * D_MODEL, (b + 1) * D_MODEL)
        logit = jnp.dot(xn, wgate_ref[:, cols], preferred_element_type=F32) + bgate_ref[:, cols]
        merged = merged + jax.nn.sigmoid(logit) * branch[b]
    out_ref[...] = x + jnp.dot(merged.astype(BF16), wo_ref[...], preferred_element_type=F32)


def _mix(x2d, xa, zu, zv, hc, ybT, g, wpool, pscale, wsgu, bsgu, convw, convb, cng, cnb,
         wgate, bgate, wpa, wpb, wpc, wpd, wo, seq):
    n = x2d.shape[0]
    tm = TOKEN_TILE
    halo_blocks = tm // HALO
    tok = lambda w: pl.BlockSpec((tm, w), lambda i: (i, 0))
    halo = lambda w: pl.BlockSpec((HALO, w), lambda i: (jnp.maximum(i * halo_blocks - 1, 0), 0))
    consts = (g, wpool, pscale, wsgu, bsgu, convw, convb, cng, cnb, wgate, bgate,
              wpa, wpb, wpc, wpd, wo)
    return pl.pallas_call(
        functools.partial(_mix_kernel, tiles_per_seq=seq // tm),
        out_shape=jax.ShapeDtypeStruct((n, D_MODEL), F32),
        grid=(n // tm,),
        in_specs=[tok(D_MODEL), tok(POOL_DIM), halo(POOL_DIM), tok(SGU_DIM), tok(SGU_DIM),
                  tok(CONV_DIM), halo(CONV_DIM),
                  pl.BlockSpec((ATT_DIM, tm), lambda i: (0, i))]
                 + [_const_spec(a.shape) for a in consts],
        out_specs=tok(D_MODEL),
        scratch_shapes=[pltpu.VMEM((tm + HALO, POOL_DIM), F32),
                        pltpu.VMEM((tm + HALO, CONV_DIM), F32)],
        compiler_params=pltpu.CompilerParams(dimension_semantics=("parallel",),
                                             vmem_limit_bytes=VMEM_LIMIT),
        name="mix_merge",
    )(x2d, xa, xa, zu, zv, hc, hc, ybT, *consts)


def _ffn_kernel(x_ref, g_ref, wg_ref, wu_ref, wout_ref, gfin_ref, out_ref, *, final_norm):
    x = x_ref[...]
    hn = _rms(x, g_ref[...]).astype(BF16)
    y = x
    for c0 in range(0, FF_DIM, FF_CHUNK):
        cols = slice(c0, c0 + FF_CHUNK)
        gate = jnp.dot(hn, wg_ref[:, cols], preferred_element_type=F32)
        up = jnp.dot(hn, wu_ref[:, cols], preferred_element_type=F32)
        act = (gate * jax.nn.sigmoid(gate) * up).astype(BF16)
        y = y + jnp.dot(act, wout_ref[cols, :], preferred_element_type=F32)
    if final_norm:
        y = _rms(y, gfin_ref[...])
    out_ref[...] = y


def _ffn(x2d, g, wg, wu, wout, gfin, final_norm):
    n = x2d.shape[0]
    tm = TOKEN_TILE
    tok = pl.BlockSpec((tm, D_MODEL), lambda i: (i, 0))
    return pl.pallas_call(
        functools.partial(_ffn_kernel, final_norm=final_norm),
        out_shape=jax.ShapeDtypeStruct((n, D_MODEL), F32),
        grid=(n // tm,),
        in_specs=[tok, _const_spec((1, D_MODEL)), _const_spec(wg.shape), _const_spec(wu.shape),
                  _const_spec(wout.shape), _const_spec((1, D_MODEL))],
        out_specs=tok,
        compiler_params=pltpu.CompilerParams(dimension_semantics=("parallel",),
                                             vmem_limit_bytes=VMEM_LIMIT),
        name="swiglu_ffn",
    )(x2d, g, wg, wu, wout, gfin)


def _block_diag(blocks):
    g, r, c = blocks.shape
    out = jnp.zeros((g * r, g * c), blocks.dtype)
    for i in range(g):
        out = out.at[i * r:(i + 1) * r, i * c:(i + 1) * c].set(blocks[i])
    return out


def kernel(x, positions, norm_mix_g, w_in, w_gate, b_gate, pool_w, pool_scale, sgu_norm_g, sgu_norm_b, sgu_w, sgu_b, conv_w, conv_b, conv_norm_g, conv_norm_b, w_proj_a, w_proj_b, w_proj_c, w_proj_d, w_o, norm_ffn_g, w_ffn_in, w_ffn_out, final_norm_g):
    batch, seq, d = x.shape
    depth = w_in.shape[0]
    n = batch * seq
    assert d == D_MODEL and seq % TOKEN_TILE == 0 and TOKEN_TILE % MOBA_BLOCK == 0
    row = lambda v: v.reshape(1, -1).astype(F32)

    inv_freq = 1.0 / (ROPE_THETA ** (jnp.arange(HALF_DIM, dtype=F32) / HALF_DIM))
    cosT, sinT = _rope_tables(positions.astype(F32).reshape(1, n), inv_freq.reshape(HALF_DIM, 1))

    causal = jnp.tril(jnp.ones((SGU_CHUNK, SGU_CHUNK), dtype=bool))
    qkv_end = POOL_DIM + 3 * ATT_DIM
    x2d = x.reshape(n, d)
    for l in range(depth):
        wqkvT = w_in[l][:, POOL_DIM:qkv_end].T.astype(BF16)
        wrest = jnp.concatenate([w_in[l][:, :POOL_DIM], w_in[l][:, qkv_end:]], axis=1).astype(BF16)
        wsgu = jnp.where(causal[None], sgu_w[l], 0.0).reshape(SGU_GROUPS * SGU_CHUNK, SGU_CHUNK)
        bsgu = jnp.repeat(sgu_b[l].T, SGU_GROUP_DIM, axis=1)

        qT, kT, vT, xa, zu, zv, hc = _inproj(
            x2d, row(norm_mix_g[l]), wqkvT, wrest, cosT, sinT,
            row(sgu_norm_g[l]), row(sgu_norm_b[l]))
        ybT = _moba(qT, kT, vT, batch, seq)
        x2d = _mix(x2d, xa, zu, zv, hc, ybT, row(norm_mix_g[l]),
                   _block_diag(pool_w[l]).astype(BF16), row(pool_scale[l]),
                   wsgu.astype(BF16), bsgu.astype(F32), conv_w[l].astype(F32), row(conv_b[l]),
                   row(conv_norm_g[l]), row(conv_norm_b[l]),
                   w_gate[l].astype(BF16), row(b_gate[l]),
                   w_proj_a[l].astype(BF16), w_proj_b[l].astype(BF16),
                   w_proj_c[l].astype(BF16), w_proj_d[l].astype(BF16),
                   w_o[l].astype(BF16), seq)
        x2d = _ffn(x2d, row(norm_ffn_g[l]), w_ffn_in[l][:, :FF_DIM].astype(BF16),
                   w_ffn_in[l][:, FF_DIM:].astype(BF16), w_ffn_out[l].astype(BF16),
                   row(final_norm_g), final_norm=(l == depth - 1))
    return x2d.reshape(batch, seq, d)
```

```python
import functools
import math

import jax
import jax.numpy as jnp
from jax import lax
from jax.experimental import pallas as pl
from jax.experimental.pallas import tpu as pltpu

F32 = jnp.float32
BF16 = jnp.bfloat16

D_MODEL = 1024
POOL_GROUPS = 4
POOL_GROUP_DIM = 64
POOL_WINDOWS = (2, 4, 8, 16)
POOL_DIM = POOL_GROUPS * POOL_GROUP_DIM
ATT_HEADS = 8
HEAD_DIM = 64
HALF_DIM = HEAD_DIM // 2
ATT_DIM = ATT_HEADS * HEAD_DIM
MOBA_BLOCK = 256
MOBA_TOPK = 3
ROPE_THETA = 10000.0
SGU_GROUPS = 4
SGU_GROUP_DIM = 64
SGU_DIM = SGU_GROUPS * SGU_GROUP_DIM
SGU_CHUNK = 128
CONV_DIM = 256
CONV_WIDTH = 31
N_BRANCH = 4
FF_DIM = 2816
EPS = 1e-6
NEG = -1e30
LOG2_E = 1.4426950408889634
BELOW_NEG = -3e38

SUBLANES = 8
HALO = 32
TOKEN_TILE = 512
FF_CHUNK = 2816
AUG = 2 * HEAD_DIM
V_ROWS = HEAD_DIM + 16
MOBA_UNROLL = 2
TRIP_HALVES = (4, 2)
VMEM_LIMIT = 56 * 1024 * 1024


def _rms(x, g):
    return x * lax.rsqrt(jnp.mean(x * x, axis=-1, keepdims=True) + EPS) * g


def _layer_norm(x, g, b):
    mu = jnp.mean(x, axis=-1, keepdims=True)
    xc = x - mu
    var = jnp.mean(xc * xc, axis=-1, keepdims=True)
    return xc * lax.rsqrt(var + EPS) * g + b


def _const_spec(shape):
    nd = len(shape)
    return pl.BlockSpec(shape, lambda *_: (0,) * nd, pipeline_mode=pl.Buffered(1))


def _rope_table_kernel(pos_ref, invf_ref, cos_ref, sin_ref):
    ang = invf_ref[...] * pos_ref[...]
    cos_ref[...] = jnp.cos(ang)
    sin_ref[...] = jnp.sin(ang)


def _rope_tables(pos_row, invf_col):
    n = pos_row.shape[1]
    tn = 2048
    return pl.pallas_call(
        _rope_table_kernel,
        out_shape=(jax.ShapeDtypeStruct((HALF_DIM, n), F32),) * 2,
        grid=(n // tn,),
        in_specs=[pl.BlockSpec((1, tn), lambda i: (0, i)),
                  pl.BlockSpec((HALF_DIM, 1), lambda i: (0, 0))],
        out_specs=(pl.BlockSpec((HALF_DIM, tn), lambda i: (0, i)),) * 2,
        name="rope_tables",
    )(pos_row, invf_col)


def _inproj_kernel(x_ref, g_ref, wqkvT_ref, wrest_ref, cos_ref, sin_ref, sgug_ref, sgub_ref,
                   qT_ref, kT_ref, vT_ref, xa_ref, zu_ref, zv_ref, hc_ref):
    xn = _rms(x_ref[...], g_ref[...]).astype(BF16)
    pT = lax.dot_general(wqkvT_ref[...], xn, (((1,), (1,)), ((), ())),
                         preferred_element_type=F32)
    c = cos_ref[...]
    s = sin_ref[...]
    for base, out_ref in ((0, qT_ref), (ATT_DIM, kT_ref)):
        for h in range(ATT_HEADS):
            r0 = base + h * HEAD_DIM
            x1 = pT[r0:r0 + HALF_DIM]
            x2 = pT[r0 + HALF_DIM:r0 + HEAD_DIM]
            o0 = h * HEAD_DIM
            out_ref[o0:o0 + HALF_DIM, :] = x1 * c - x2 * s
            out_ref[o0 + HALF_DIM:o0 + HEAD_DIM, :] = x2 * c + x1 * s
    vT_ref[...] = pT[2 * ATT_DIM:].astype(BF16)

    r = jnp.dot(xn, wrest_ref[...], preferred_element_type=F32)
    xa_ref[...] = r[:, :POOL_DIM]
    o = POOL_DIM
    zu_ref[...] = jax.nn.gelu(r[:, o:o + SGU_DIM], approximate=True)
    zv = jax.nn.gelu(r[:, o + SGU_DIM:o + 2 * SGU_DIM], approximate=True)
    zv_ref[...] = _layer_norm(zv, sgug_ref[...], sgub_ref[...]).astype(BF16)
    o += 2 * SGU_DIM
    hc_ref[...] = r[:, o:o + CONV_DIM] * jax.nn.sigmoid(r[:, o + CONV_DIM:o + 2 * CONV_DIM])


def _inproj(x2d, g, wqkvT, wrest, cosT, sinT, sgug, sgub):
    n = x2d.shape[0]
    tm = TOKEN_TILE
    n_rest = wrest.shape[1]
    tok = lambda w: pl.BlockSpec((tm, w), lambda i: (i, 0))
    feat = lambda r: pl.BlockSpec((r, tm), lambda i: (0, i))
    return pl.pallas_call(
        _inproj_kernel,
        out_shape=(jax.ShapeDtypeStruct((ATT_DIM, n), F32),
                   jax.ShapeDtypeStruct((ATT_DIM, n), F32),
                   jax.ShapeDtypeStruct((ATT_DIM, n), BF16),
                   jax.ShapeDtypeStruct((n, POOL_DIM), F32),
                   jax.ShapeDtypeStruct((n, SGU_DIM), F32),
                   jax.ShapeDtypeStruct((n, SGU_DIM), BF16),
                   jax.ShapeDtypeStruct((n, CONV_DIM), F32)),
        grid=(n // tm,),
        in_specs=[tok(D_MODEL), _const_spec((1, D_MODEL)),
                  _const_spec((3 * ATT_DIM, D_MODEL)), _const_spec((D_MODEL, n_rest)),
                  feat(HALF_DIM), feat(HALF_DIM),
                  _const_spec((1, SGU_DIM)), _const_spec((1, SGU_DIM))],
        out_specs=(feat(ATT_DIM), feat(ATT_DIM), feat(ATT_DIM),
                   tok(POOL_DIM), tok(SGU_DIM), tok(SGU_DIM), tok(CONV_DIM)),
        compiler_params=pltpu.CompilerParams(dimension_semantics=("parallel",),
                                             vmem_limit_bytes=VMEM_LIMIT),
        name="inproj",
    )(x2d, g, wqkvT, wrest, cosT, sinT, sgug, sgub)


def _moba_kernel(qT_ref, kT_ref, vT_ref, oT_ref, kaug_s, vaug_s, ksum_s, *stage, nb, unroll):
    blk = MOBA_BLOCK
    hot_rows = lax.broadcasted_iota(jnp.int32, (HEAD_DIM, blk), 0)
    ones_row = (lax.broadcasted_iota(jnp.int32, (V_ROWS - HEAD_DIM, blk), 0) == 0).astype(BF16)
    for j in range(nb):
        kTj = kT_ref[:, j * blk:(j + 1) * blk]
        kaT = jnp.concatenate([kTj, (hot_rows == j).astype(F32)], axis=0)
        ka = kaT.T
        kaug_s[j * blk:(j + 1) * blk, :] = ka.astype(BF16)
        ksum_s[j:j + 1, :] = jnp.sum(ka, axis=0, keepdims=True)
        vaug_s[j] = jnp.concatenate([vT_ref[:, j * blk:(j + 1) * blk], ones_row], axis=0)

    def query_block(i, _):
        _moba_query_block(i, qT_ref, oT_ref, kaug_s, vaug_s, ksum_s, stage, nb, unroll)
        return 0

    lax.fori_loop(0, nb, query_block, 0)


def _moba_query_block(i, qT_ref, oT_ref, kaug_s, vaug_s, ksum_s, stage, nb, unroll):
    blk = MOBA_BLOCK
    q_cols = pl.ds(pl.multiple_of(i * blk, blk), blk)
    qT = qT_ref[:, q_cols]
    kmean = ksum_s[...] * (1.0 / blk)
    gate = jnp.dot(kmean, jnp.concatenate([qT, jnp.zeros((HEAD_DIM, blk), F32)], axis=0),
                   preferred_element_type=F32, precision=lax.Precision.HIGHEST)
    rows = lax.broadcasted_iota(jnp.int32, (nb, blk), 0)
    past = rows < i
    g = jnp.where(past, gate, NEG)
    sel = jnp.zeros((nb, blk), jnp.bool_)
    for _ in range(MOBA_TOPK):
        top = jnp.max(g, axis=0, keepdims=True)
        first = jnp.min(jnp.where(g == top, rows, nb), axis=0, keepdims=True)
        hit = rows == first
        sel = jnp.logical_or(sel, hit)
        g = jnp.where(hit, BELOW_NEG, g)
    sel = jnp.logical_and(sel, past)
    bias_f = jnp.where(sel, 0.0, NEG)
    bias = bias_f.astype(BF16)
    if nb < HEAD_DIM:
        bias = jnp.concatenate([bias, jnp.zeros((HEAD_DIM - nb, blk), BF16)], axis=0)
    q_bf = (qT * (HEAD_DIM ** -0.5 * LOG2_E)).astype(BF16)
    q_aug = jnp.concatenate([q_bf, bias], axis=0)
    q_own = jnp.concatenate([q_bf, jnp.zeros((HEAD_DIM, blk), BF16)], axis=0)
    grp = unroll * blk

    k_own = kaug_s[pl.ds(pl.multiple_of(i * blk, blk), blk), :]
    s_own = jnp.dot(k_own, q_own, preferred_element_type=F32)
    causal = (lax.broadcasted_iota(jnp.int32, (blk, blk), 0)
              <= lax.broadcasted_iota(jnp.int32, (blk, blk), 1))
    s_own = jnp.where(causal, s_own, NEG)
    m_own = jnp.max(s_own, axis=0, keepdims=True)
    p_own = jnp.exp2(s_own - m_own).astype(BF16)
    carry = (m_own, jnp.dot(vaug_s[i], p_own, preferred_element_type=F32))

    def scores(gi, dst):
        rows_g = kaug_s[pl.ds(pl.multiple_of(gi * grp, grp), grp), :]
        dst[...] = jnp.dot(rows_g, q_aug, preferred_element_type=F32)

    def attend(gi, src, cr):
        maxes, outs = [], []
        for u in range(unroll):
            tile = slice(u * blk, (u + 1) * blk)
            m_u = jnp.max(src[tile, :], axis=0, keepdims=True)
            p = jnp.exp2(src[tile, :] - m_u).astype(BF16)
            outs.append(jnp.dot(vaug_s[gi * unroll + u], p, preferred_element_type=F32))
            maxes.append(m_u)
        m, acc = cr
        m_new = m
        for m_u in maxes:
            m_new = jnp.maximum(m_new, m_u)
        acc = acc * jnp.exp2(m - m_new)
        for m_u, o_u in zip(maxes, outs):
            acc = acc + o_u * jnp.exp2(m_u - m_new)
        return m_new, acc

    n_groups = (i + unroll - 1) // unroll
    last_group = nb // unroll - 1
    raw = jnp.dot(kaug_s[0:2 * grp, :], q_own, preferred_element_type=F32)
    for b in range(2 * unroll):
        k, u = divmod(b, unroll)
        stage[k][u * blk:(u + 1) * blk, :] = raw[b * blk:(b + 1) * blk] + bias_f[b:b + 1, :]

    def trip(g0, halves, cr):
        for half in range(halves):
            cur, nxt = (stage[:2], stage[2:]) if half % 2 == 0 else (stage[2:], stage[:2])
            base = g0 + 2 * half
            for k in range(2):
                scores(jnp.minimum(base + 2 + k, last_group), nxt[k])
            for k in range(2):
                cr = attend(base + k, cur[k], cr)
        return cr

    done = 0
    for size_idx, halves in enumerate(TRIP_HALVES):
        groups = 2 * halves
        n_trips = (n_groups - done) // groups
        if size_idx + 1 == len(TRIP_HALVES):
            n_trips = n_trips + ((n_groups - done) % groups > 2).astype(jnp.int32)
        carry = lax.fori_loop(
            0, n_trips,
            lambda t, cr, done=done, groups=groups, halves=halves: trip(done + groups * t, halves, cr),
            carry)
        done = done + groups * n_trips
    carry = lax.cond(
        done < n_groups,
        lambda cr: attend(done + 1, stage[1], attend(done, stage[0], cr)),
        lambda cr: cr, carry)
    acc = carry[1]
    oT_ref[:, q_cols] = (acc[:HEAD_DIM] / acc[HEAD_DIM:HEAD_DIM + 1]).astype(oT_ref.dtype)


def _moba(qT, kT, vT, batch, seq):
    nb = seq // MOBA_BLOCK
    assert nb <= HEAD_DIM, "one-hot block id must fit beside the head features"
    blk = MOBA_BLOCK
    assert nb % 4 == 0
    unroll = math.gcd(nb // 4, MOBA_UNROLL)
    return pl.pallas_call(
        functools.partial(_moba_kernel, nb=nb, unroll=unroll),
        out_shape=jax.ShapeDtypeStruct((ATT_DIM, batch * seq), BF16),
        grid=(batch, ATT_HEADS),
        in_specs=[pl.BlockSpec((HEAD_DIM, seq), lambda b, h: (h, b))] * 3,
        out_specs=pl.BlockSpec((HEAD_DIM, seq), lambda b, h: (h, b)),
        scratch_shapes=[pltpu.VMEM((nb * blk, AUG), BF16),
                        pltpu.VMEM((nb, V_ROWS, blk), BF16),
                        pltpu.VMEM((nb, AUG), F32)]
                       + [pltpu.VMEM((unroll * blk, blk), F32)] * 4,
        compiler_params=pltpu.CompilerParams(
            dimension_semantics=("arbitrary", "arbitrary"),
            vmem_limit_bytes=VMEM_LIMIT),
        name="moba",
    )(qT, kT, vT)


def _mix_kernel(x_ref, xa_ref, xa_halo_ref, zu_ref, zv_ref, hc_ref, hc_halo_ref, ybT_ref,
                g_ref, wpool_ref, pscale_ref, wsgu_ref, bsgu_ref, convw_ref, convb_ref,
                cng_ref, cnb_ref, wgate_ref, bgate_ref, wpa_ref, wpb_ref, wpc_ref, wpd_ref,
                wo_ref, out_ref, xa_ext, hc_ext, *, tiles_per_seq):
    tm = x_ref.shape[0]
    i = pl.program_id(0)
    seq_tile = i % tiles_per_seq
    keep_halo = (seq_tile > 0).astype(F32)

    xa = xa_ref[...]
    xa_ext[0:HALO, :] = xa_halo_ref[...] * keep_halo
    xa_ext[HALO:, :] = xa
    lane = lax.broadcasted_iota(jnp.int32, (tm, POOL_DIM), 1)
    t_seq = seq_tile * tm + lax.broadcasted_iota(jnp.int32, (tm, POOL_DIM), 0)
    shifted = lambda d: xa_ext[HALO - d:HALO - d + tm, :]
    wsum = xa
    win = jnp.zeros((tm, POOL_DIM), F32)
    done = 1
    for gi, w in enumerate(POOL_WINDOWS):
        for d in range(done, w):
            wsum = wsum + shifted(d)
        done = w
        in_group = (lane // POOL_GROUP_DIM) == gi
        cnt = jnp.minimum(t_seq + 1, w).astype(F32)
        win = jnp.where(in_group, wsum / cnt, win)
    pooled = (win - xa).astype(BF16)
    ya = jnp.dot(pooled, wpool_ref[...], preferred_element_type=F32) * pscale_ref[...]

    lane_c = lax.broadcasted_iota(jnp.int32, (SGU_CHUNK, SGU_DIM), 1) // SGU_GROUP_DIM
    yc_parts = []
    for c in range(tm // SGU_CHUNK):
        rs = slice(c * SGU_CHUNK, (c + 1) * SGU_CHUNK)
        f_all = jnp.dot(wsgu_ref[...], zv_ref[rs, :], preferred_element_type=F32)
        f = f_all[0:SGU_CHUNK]
        for gi in range(1, SGU_GROUPS):
            f = jnp.where(lane_c == gi, f_all[gi * SGU_CHUNK:(gi + 1) * SGU_CHUNK], f)
        yc_parts.append(zu_ref[rs, :] * (f + bsgu_ref[...]))
    yc = jnp.concatenate(yc_parts, axis=0)

    hc_ext[0:HALO, :] = hc_halo_ref[...] * keep_halo
    hc_ext[HALO:, :] = hc_ref[...]
    rows = 64
    first_off = HALO - (CONV_WIDTH - 1)
    yd_parts = []
    for r0 in range(0, tm, rows):
        acc = None
        for shift in range(SUBLANES):
            slab = None
            for off in range(shift, first_off + CONV_WIDTH, SUBLANES):
                j = off - first_off
                if j < 0:
                    continue
                base = r0 + off - shift
                if shift == 0:
                    term = hc_ext[base:base + rows, :] * convw_ref[j:j + 1, :]
                else:
                    term = hc_ext[base:base + rows + SUBLANES, :] * convw_ref[j:j + 1, :]
                slab = term if slab is None else slab + term
            part = slab if shift == 0 else slab[shift:shift + rows]
            acc = part if acc is None else acc + part
        yd_parts.append(acc)
    conv = jnp.concatenate(yd_parts, axis=0) + convb_ref[...]
    yd = _layer_norm(conv, cng_ref[...], cnb_ref[...])
    yd = yd * jax.nn.sigmoid(yd)

    x = x_ref[...]
    xn = _rms(x, g_ref[...]).astype(BF16)
    branch = (
        jnp.dot(ya.astype(BF16), wpa_ref[...], preferred_element_type=F32),
        lax.dot_general(ybT_ref[...], wpb_ref[...], (((0,), (0,)), ((), ())),
                        preferred_element_type=F32),
        jnp.dot(yc.astype(BF16), wpc_ref[...], preferred_element_type=F32),
        jnp.dot(yd.astype(BF16), wpd_ref[...], preferred_element_type=F32),
    )
    merged = jnp.zeros((tm, D_MODEL), F32)
    for b in range(N_BRANCH):
        cols = slice(b * D_MODEL, (b + 1) * D_MODEL)
        logit = jnp.dot(xn, wgate_ref[:, cols], preferred_element_type=F32) + bgate_ref[:, cols]
        merged = merged + jax.nn.sigmoid(logit) * branch[b]
    out_ref[...] = x + jnp.dot(merged.astype(BF16), wo_ref[...], preferred_element_type=F32)


def _mix(x2d, xa, zu, zv, hc, ybT, g, wpool, pscale, wsgu, bsgu, convw, convb, cng, cnb,
         wgate, bgate, wpa, wpb, wpc, wpd, wo, seq):
    n = x2d.shape[0]
    tm = TOKEN_TILE
    halo_blocks = tm // HALO
    tok = lambda w: pl.BlockSpec((tm, w), lambda i: (i, 0))
    halo = lambda w: pl.BlockSpec((HALO, w), lambda i: (jnp.maximum(i * halo_blocks - 1, 0), 0))
    consts = (g, wpool, pscale, wsgu, bsgu, convw, convb, cng, cnb, wgate, bgate,
              wpa, wpb, wpc, wpd, wo)
    return pl.pallas_call(
        functools.partial(_mix_kernel, tiles_per_seq=seq // tm),
        out_shape=jax.ShapeDtypeStruct((n, D_MODEL), F32),
        grid=(n // tm,),
        in_specs=[tok(D_MODEL), tok(POOL_DIM), halo(POOL_DIM), tok(SGU_DIM), tok(SGU_DIM),
                  tok(CONV_DIM), halo(CONV_DIM),
                  pl.BlockSpec((ATT_DIM, tm), lambda i: (0, i))]
                 + [_const_spec(a.shape) for a in consts],
        out_specs=tok(D_MODEL),
        scratch_shapes=[pltpu.VMEM((tm + HALO, POOL_DIM), F32),
                        pltpu.VMEM((tm + HALO, CONV_DIM), F32)],
        compiler_params=pltpu.CompilerParams(dimension_semantics=("parallel",),
                                             vmem_limit_bytes=VMEM_LIMIT),
        name="mix_merge",
    )(x2d, xa, xa, zu, zv, hc, hc, ybT, *consts)


def _ffn_kernel(x_ref, g_ref, wg_ref, wu_ref, wout_ref, gfin_ref, out_ref, *, final_norm):
    x = x_ref[...]
    hn = _rms(x, g_ref[...]).astype(BF16)
    y = x
    for c0 in range(0, FF_DIM, FF_CHUNK):
        cols = slice(c0, c0 + FF_CHUNK)
        gate = jnp.dot(hn, wg_ref[:, cols], preferred_element_type=F32)
        up = jnp.dot(hn, wu_ref[:, cols], preferred_element_type=F32)
        act = (gate * jax.nn.sigmoid(gate) * up).astype(BF16)
        y = y + jnp.dot(act, wout_ref[cols, :], preferred_element_type=F32)
    if final_norm:
        y = _rms(y, gfin_ref[...])
    out_ref[...] = y


def _ffn(x2d, g, wg, wu, wout, gfin, final_norm):
    n = x2d.shape[0]
    tm = TOKEN_TILE
    tok = pl.BlockSpec((tm, D_MODEL), lambda i: (i, 0))
    return pl.pallas_call(
        functools.partial(_ffn_kernel, final_norm=final_norm),
        out_shape=jax.ShapeDtypeStruct((n, D_MODEL), F32),
        grid=(n // tm,),
        in_specs=[tok, _const_spec((1, D_MODEL)), _const_spec(wg.shape), _const_spec(wu.shape),
                  _const_spec(wout.shape), _const_spec((1, D_MODEL))],
        out_specs=tok,
        compiler_params=pltpu.CompilerParams(dimension_semantics=("parallel",),
                                             vmem_limit_bytes=VMEM_LIMIT),
        name="swiglu_ffn",
    )(x2d, g, wg, wu, wout, gfin)


def _block_diag(blocks):
    g, r, c = blocks.shape
    out = jnp.zeros((g * r, g * c), blocks.dtype)
    for i in range(g):
        out = out.at[i * r:(i + 1) * r, i * c:(i + 1) * c].set(blocks[i])
    return out


def kernel(x, positions, norm_mix_g, w_in, w_gate, b_gate, pool_w, pool_scale, sgu_norm_g, sgu_norm_b, sgu_w, sgu_b, conv_w, conv_b, conv_norm_g, conv_norm_b, w_proj_a, w_proj_b, w_proj_c, w_proj_d, w_o, norm_ffn_g, w_ffn_in, w_ffn_out, final_norm_g):
    batch, seq, d = x.shape
    depth = w_in.shape[0]
    n = batch * seq
    assert d == D_MODEL and seq % TOKEN_TILE == 0 and TOKEN_TILE % MOBA_BLOCK == 0
    row = lambda v: v.reshape(1, -1).astype(F32)

    inv_freq = 1.0 / (ROPE_THETA ** (jnp.arange(HALF_DIM, dtype=F32) / HALF_DIM))
    cosT, sinT = _rope_tables(positions.astype(F32).reshape(1, n), inv_freq.reshape(HALF_DIM, 1))

    causal = jnp.tril(jnp.ones((SGU_CHUNK, SGU_CHUNK), dtype=bool))
    qkv_end = POOL_DIM + 3 * ATT_DIM
    x2d = x.reshape(n, d)
    for l in range(depth):
        wqkvT = w_in[l][:, POOL_DIM:qkv_end].T.astype(BF16)
        wrest = jnp.concatenate([w_in[l][:, :POOL_DIM], w_in[l][:, qkv_end:]], axis=1).astype(BF16)
        wsgu = jnp.where(causal[None], sgu_w[l], 0.0).reshape(SGU_GROUPS * SGU_CHUNK, SGU_CHUNK)
        bsgu = jnp.repeat(sgu_b[l].T, SGU_GROUP_DIM, axis=1)

        qT, kT, vT, xa, zu, zv, hc = _inproj(
            x2d, row(norm_mix_g[l]), wqkvT, wrest, cosT, sinT,
            row(sgu_norm_g[l]), row(sgu_norm_b[l]))
        ybT = _moba(qT, kT, vT, batch, seq)
        x2d = _mix(x2d, xa, zu, zv, hc, ybT, row(norm_mix_g[l]),
                   _block_diag(pool_w[l]).astype(BF16), row(pool_scale[l]),
                   wsgu.astype(BF16), bsgu.astype(F32), conv_w[l].astype(F32), row(conv_b[l]),
                   row(conv_norm_g[l]), row(conv_norm_b[l]),
                   w_gate[l].astype(BF16), row(b_gate[l]),
                   w_proj_a[l].astype(BF16), w_proj_b[l].astype(BF16),
                   w_proj_c[l].astype(BF16), w_proj_d[l].astype(BF16),
                   w_o[l].astype(BF16), seq)
        x2d = _ffn(x2d, row(norm_ffn_g[l]), w_ffn_in[l][:, :FF_DIM].astype(BF16),
                   w_ffn_in[l][:, FF_DIM:].astype(BF16), w_ffn_out[l].astype(BF16),
                   row(final_norm_g), final_norm=(l == depth - 1))
    return x2d.reshape(batch, seq, d)
```

```python
import functools
import math

import jax
import jax.numpy as jnp
from jax import lax
from jax.experimental import pallas as pl
from jax.experimental.pallas import tpu as pltpu

F32 = jnp.float32
BF16 = jnp.bfloat16

D_MODEL = 1024
POOL_GROUPS = 4
POOL_GROUP_DIM = 64
POOL_WINDOWS = (2, 4, 8, 16)
POOL_DIM = POOL_GROUPS * POOL_GROUP_DIM
ATT_HEADS = 8
HEAD_DIM = 64
HALF_DIM = HEAD_DIM // 2
ATT_DIM = ATT_HEADS * HEAD_DIM
MOBA_BLOCK = 256
MOBA_TOPK = 3
ROPE_THETA = 10000.0
SGU_GROUPS = 4
SGU_GROUP_DIM = 64
SGU_DIM = SGU_GROUPS * SGU_GROUP_DIM
SGU_CHUNK = 128
CONV_DIM = 256
CONV_WIDTH = 31
N_BRANCH = 4
FF_DIM = 2816
EPS = 1e-6
NEG = -1e30
LOG2_E = 1.4426950408889634
BELOW_NEG = -3e38

SUBLANES = 8
HALO = 32
TOKEN_TILE = 512
FF_CHUNK = 2816
AUG = 2 * HEAD_DIM
V_ROWS = HEAD_DIM + 16
MOBA_UNROLL = 2
TRIP_HALVES = (4, 2)
VMEM_LIMIT = 56 * 1024 * 1024


def _rms(x, g):
    return x * lax.rsqrt(jnp.mean(x * x, axis=-1, keepdims=True) + EPS) * g


def _layer_norm(x, g, b):
    mu = jnp.mean(x, axis=-1, keepdims=True)
    xc = x - mu
    var = jnp.mean(xc * xc, axis=-1, keepdims=True)
    return xc * lax.rsqrt(var + EPS) * g + b


def _const_spec(shape):
    nd = len(shape)
    return pl.BlockSpec(shape, lambda *_: (0,) * nd, pipeline_mode=pl.Buffered(1))


def _rope_table_kernel(pos_ref, invf_ref, cos_ref, sin_ref):
    ang = invf_ref[...] * pos_ref[...]
    cos_ref[...] = jnp.cos(ang)
    sin_ref[...] = jnp.sin(ang)


def _rope_tables(pos_row, invf_col):
    n = pos_row.shape[1]
    tn = 2048
    return pl.pallas_call(
        _rope_table_kernel,
        out_shape=(jax.ShapeDtypeStruct((HALF_DIM, n), F32),) * 2,
        grid=(n // tn,),
        in_specs=[pl.BlockSpec((1, tn), lambda i: (0, i)),
                  pl.BlockSpec((HALF_DIM, 1), lambda i: (0, 0))],
        out_specs=(pl.BlockSpec((HALF_DIM, tn), lambda i: (0, i)),) * 2,
        name="rope_tables",
    )(pos_row, invf_col)


def _inproj_kernel(x_ref, g_ref, wqkvT_ref, wrest_ref, cos_ref, sin_ref, sgug_ref, sgub_ref,
                   qT_ref, kT_ref, vT_ref, xa_ref, zu_ref, zv_ref, hc_ref):
    xn = _rms(x_ref[...], g_ref[...]).astype(BF16)
    pT = lax.dot_general(wqkvT_ref[...], xn, (((1,), (1,)), ((), ())),
                         preferred_element_type=F32)
    c = cos_ref[...]
    s = sin_ref[...]
    for base, out_ref in ((0, qT_ref), (ATT_DIM, kT_ref)):
        for h in range(ATT_HEADS):
            r0 = base + h * HEAD_DIM
            x1 = pT[r0:r0 + HALF_DIM]
            x2 = pT[r0 + HALF_DIM:r0 + HEAD_DIM]
            o0 = h * HEAD_DIM
            out_ref[o0:o0 + HALF_DIM, :] = x1 * c - x2 * s
            out_ref[o0 + HALF_DIM:o0 + HEAD_DIM, :] = x2 * c + x1 * s
    vT_ref[...] = pT[2 * ATT_DIM:].astype(BF16)

    r = jnp.dot(xn, wrest_ref[...], preferred_element_type=F32)
    xa_ref[...] = r[:, :POOL_DIM]
    o = POOL_DIM
    zu_ref[...] = jax.nn.gelu(r[:, o:o + SGU_DIM], approximate=True)
    zv = jax.nn.gelu(r[:, o + SGU_DIM:o + 2 * SGU_DIM], approximate=True)
    zv_ref[...] = _layer_norm(zv, sgug_ref[...], sgub_ref[...]).astype(BF16)
    o += 2 * SGU_DIM
    hc_ref[...] = r[:, o:o + CONV_DIM] * jax.nn.sigmoid(r[:, o + CONV_DIM:o + 2 * CONV_DIM])


def _inproj(x2d, g, wqkvT, wrest, cosT, sinT, sgug, sgub):
    n = x2d.shape[0]
    tm = TOKEN_TILE
    n_rest = wrest.shape[1]
    tok = lambda w: pl.BlockSpec((tm, w), lambda i: (i, 0))
    feat = lambda r: pl.BlockSpec((r, tm), lambda i: (0, i))
    return pl.pallas_call(
        _inproj_kernel,
        out_shape=(jax.ShapeDtypeStruct((ATT_DIM, n), F32),
                   jax.ShapeDtypeStruct((ATT_DIM, n), F32),
                   jax.ShapeDtypeStruct((ATT_DIM, n), BF16),
                   jax.ShapeDtypeStruct((n, POOL_DIM), F32),
                   jax.ShapeDtypeStruct((n, SGU_DIM), F32),
                   jax.ShapeDtypeStruct((n, SGU_DIM), BF16),
                   jax.ShapeDtypeStruct((n, CONV_DIM), F32)),
        grid=(n // tm,),
        in_specs=[tok(D_MODEL), _const_spec((1, D_MODEL)),
                  _const_spec((3 * ATT_DIM, D_MODEL)), _const_spec((D_MODEL, n_rest)),
                  feat(HALF_DIM), feat(HALF_DIM),
                  _const_spec((1, SGU_DIM)), _const_spec((1, SGU_DIM))],
        out_specs=(feat(ATT_DIM), feat(ATT_DIM), feat(ATT_DIM),
                   tok(POOL_DIM), tok(SGU_DIM), tok(SGU_DIM), tok(CONV_DIM)),
        compiler_params=pltpu.CompilerParams(dimension_semantics=("parallel",),
                                             vmem_limit_bytes=VMEM_LIMIT),
        name="inproj",
    )(x2d, g, wqkvT, wrest, cosT, sinT, sgug, sgub)


def _moba_kernel(qT_ref, kT_ref, vT_ref, oT_ref, kaug_s, vaug_s, ksum_s, *stage, nb, unroll):
    blk = MOBA_BLOCK
    hot_rows = lax.broadcasted_iota(jnp.int32, (HEAD_DIM, blk), 0)
    ones_row = (lax.broadcasted_iota(jnp.int32, (V_ROWS - HEAD_DIM, blk), 0) == 0).astype(BF16)
    for j in range(nb):
        kTj = kT_ref[:, j * blk:(j + 1) * blk]
        kaT = jnp.concatenate([kTj, (hot_rows == j).astype(F32)], axis=0)
        ka = kaT.T
        kaug_s[j * blk:(j + 1) * blk, :] = ka.astype(BF16)
        ksum_s[j:j + 1, :] = jnp.sum(ka, axis=0, keepdims=True)
        vaug_s[j] = jnp.concatenate([vT_ref[:, j * blk:(j + 1) * blk], ones_row], axis=0)

    def query_block(i, _):
        _moba_query_block(i, qT_ref, oT_ref, kaug_s, vaug_s, ksum_s, stage, nb, unroll)
        return 0

    lax.fori_loop(0, nb, query_block, 0)


def _moba_query_block(i, qT_ref, oT_ref, kaug_s, vaug_s, ksum_s, stage, nb, unroll):
    blk = MOBA_BLOCK
    q_cols = pl.ds(pl.multiple_of(i * blk, blk), blk)
    qT = qT_ref[:, q_cols]
    kmean = ksum_s[...] * (1.0 / blk)
    q_pad = jnp.concatenate([qT, jnp.zeros((HEAD_DIM, blk), F32)], axis=0)
    km_hi = kmean.astype(BF16)
    km_lo = (kmean - km_hi.astype(F32)).astype(BF16)
    q_hi = q_pad.astype(BF16)
    q_lo = (q_pad - q_hi.astype(F32)).astype(BF16)
    gate = jnp.dot(jnp.concatenate([km_hi, km_lo, km_hi], axis=1),
                   jnp.concatenate([q_hi, q_hi, q_lo], axis=0), preferred_element_type=F32)
    rows = lax.broadcasted_iota(jnp.int32, (nb, blk), 0)
    past = rows < i
    g = jnp.where(past, gate, NEG)
    sel = jnp.zeros((nb, blk), jnp.bool_)
    for _ in range(MOBA_TOPK):
        top = jnp.max(g, axis=0, keepdims=True)
        first = jnp.min(jnp.where(g == top, rows, nb), axis=0, keepdims=True)
        hit = rows == first
        sel = jnp.logical_or(sel, hit)
        g = jnp.where(hit, BELOW_NEG, g)
    sel = jnp.logical_and(sel, past)
    bias_f = jnp.where(sel, 0.0, NEG)
    bias = bias_f.astype(BF16)
    if nb < HEAD_DIM:
        bias = jnp.concatenate([bias, jnp.zeros((HEAD_DIM - nb, blk), BF16)], axis=0)
    q_bf = (qT * (HEAD_DIM ** -0.5 * LOG2_E)).astype(BF16)
    q_aug = jnp.concatenate([q_bf, bias], axis=0)
    q_own = jnp.concatenate([q_bf, jnp.zeros((HEAD_DIM, blk), BF16)], axis=0)
    grp = unroll * blk

    k_own = kaug_s[pl.ds(pl.multiple_of(i * blk, blk), blk), :]
    s_own = jnp.dot(k_own, q_own, preferred_element_type=F32)
    causal = (lax.broadcasted_iota(jnp.int32, (blk, blk), 0)
              <= lax.broadcasted_iota(jnp.int32, (blk, blk), 1))
    s_own = jnp.where(causal, s_own, NEG)
    m_own = jnp.max(s_own, axis=0, keepdims=True)
    p_own = jnp.exp2(s_own - m_own).astype(BF16)
    carry = (m_own, jnp.dot(vaug_s[i], p_own, preferred_element_type=F32))

    def scores(gi, dst):
        rows_g = kaug_s[pl.ds(pl.multiple_of(gi * grp, grp), grp), :]
        dst[...] = jnp.dot(rows_g, q_aug, preferred_element_type=F32)

    def attend(gi, src, cr):
        maxes, outs = [], []
        for u in range(unroll):
            tile = slice(u * blk, (u + 1) * blk)
            m_u = jnp.max(src[tile, :], axis=0, keepdims=True)
            p = jnp.exp2(src[tile, :] - m_u).astype(BF16)
            outs.append(jnp.dot(vaug_s[gi * unroll + u], p, preferred_element_type=F32))
            maxes.append(m_u)
        m, acc = cr
        m_new = m
        for m_u in maxes:
            m_new = jnp.maximum(m_new, m_u)
        acc = acc * jnp.exp2(m - m_new)
        for m_u, o_u in zip(maxes, outs):
            acc = acc + o_u * jnp.exp2(m_u - m_new)
        return m_new, acc

    n_groups = (i + unroll - 1) // unroll
    last_group = nb // unroll - 1
    for k in range(2):
        stage[k][...] = jnp.dot(kaug_s[k * grp:(k + 1) * grp, :], q_own, preferred_element_type=F32)
    for k in range(2):
        for u in range(unroll):
            b = k * unroll + u
            tile = slice(u * blk, (u + 1) * blk)
            stage[k][tile, :] = stage[k][tile, :] + bias_f[b:b + 1, :]

    def trip(g0, halves, cr):
        for half in range(halves):
            cur, nxt = (stage[:2], stage[2:]) if half % 2 == 0 else (stage[2:], stage[:2])
            base = g0 + 2 * half
            for k in range(2):
                scores(jnp.minimum(base + 2 + k, last_group), nxt[k])
            for k in range(2):
                cr = attend(base + k, cur[k], cr)
        return cr

    done = 0
    for size_idx, halves in enumerate(TRIP_HALVES):
        groups = 2 * halves
        n_trips = (n_groups - done) // groups
        if size_idx + 1 == len(TRIP_HALVES):
            n_trips = n_trips + jnp.where((n_groups - done) % groups > 2, 1, 0)
        carry = lax.fori_loop(
            0, n_trips,
            lambda t, cr, done=done, groups=groups, halves=halves: trip(done + groups * t, halves, cr),
            carry)
        done = done + groups * n_trips
    carry = lax.cond(
        done < n_groups,
        lambda cr: attend(done + 1, stage[1], attend(done, stage[0], cr)),
        lambda cr: cr, carry)
    acc = carry[1]
    oT_ref[:, q_cols] = (acc[:HEAD_DIM] / acc[HEAD_DIM:HEAD_DIM + 1]).astype(oT_ref.dtype)


def _moba(qT, kT, vT, batch, seq):
    nb = seq // MOBA_BLOCK
    assert nb <= HEAD_DIM, "one-hot block id must fit beside the head features"
    blk = MOBA_BLOCK
    assert nb % 4 == 0
    unroll = math.gcd(nb // 4, MOBA_UNROLL)
    return pl.pallas_call(
        functools.partial(_moba_kernel, nb=nb, unroll=unroll),
        out_shape=jax.ShapeDtypeStruct((ATT_DIM, batch * seq), BF16),
        grid=(batch, ATT_HEADS),
        in_specs=[pl.BlockSpec((HEAD_DIM, seq), lambda b, h: (h, b))] * 3,
        out_specs=pl.BlockSpec((HEAD_DIM, seq), lambda b, h: (h, b)),
        scratch_shapes=[pltpu.VMEM((nb * blk, AUG), BF16),
                        pltpu.VMEM((nb, V_ROWS, blk), BF16),
                        pltpu.VMEM((nb, AUG), F32)]
                       + [pltpu.VMEM((unroll * blk, blk), F32)] * 4,
        compiler_params=pltpu.CompilerParams(
            dimension_semantics=("arbitrary", "arbitrary"),
            vmem_limit_bytes=VMEM_LIMIT),
        name="moba",
    )(qT, kT, vT)


def _mix_kernel(x_ref, xa_ref, xa_halo_ref, zu_ref, zv_ref, hc_ref, hc_halo_ref, ybT_ref,
                g_ref, wpool_ref, pscale_ref, wsgu_ref, bsgu_ref, convw_ref, convb_ref,
                cng_ref, cnb_ref, wgate_ref, bgate_ref, wpa_ref, wpb_ref, wpc_ref, wpd_ref,
                wo_ref, out_ref, xa_ext, hc_ext, *, tiles_per_seq):
    tm = x_ref.shape[0]
    i = pl.program_id(0)
    seq_tile = i % tiles_per_seq
    keep_halo = (seq_tile > 0).astype(F32)

    xa = xa_ref[...]
    xa_ext[0:HALO, :] = xa_halo_ref[...] * keep_halo
    xa_ext[HALO:, :] = xa
    lane = lax.broadcasted_iota(jnp.int32, (tm, POOL_DIM), 1)
    t_seq = seq_tile * tm + lax.broadcasted_iota(jnp.int32, (tm, POOL_DIM), 0)
    shifted = lambda d: xa_ext[HALO - d:HALO - d + tm, :]
    wsum = xa
    win = jnp.zeros((tm, POOL_DIM), F32)
    done = 1
    for gi, w in enumerate(POOL_WINDOWS):
        for d in range(done, w):
            wsum = wsum + shifted(d)
        done = w
        in_group = (lane // POOL_GROUP_DIM) == gi
        cnt = jnp.minimum(t_seq + 1, w).astype(F32)
        win = jnp.where(in_group, wsum / cnt, win)
    pooled = (win - xa).astype(BF16)
    ya = jnp.dot(pooled, wpool_ref[...], preferred_element_type=F32) * pscale_ref[...]

    lane_c = lax.broadcasted_iota(jnp.int32, (SGU_CHUNK, SGU_DIM), 1) // SGU_GROUP_DIM
    yc_parts = []
    for c in range(tm // SGU_CHUNK):
        rs = slice(c * SGU_CHUNK, (c + 1) * SGU_CHUNK)
        f_all = jnp.dot(wsgu_ref[...], zv_ref[rs, :], preferred_element_type=F32)
        f = f_all[0:SGU_CHUNK]
        for gi in range(1, SGU_GROUPS):
            f = jnp.where(lane_c == gi, f_all[gi * SGU_CHUNK:(gi + 1) * SGU_CHUNK], f)
        yc_parts.append(zu_ref[rs, :] * (f + bsgu_ref[...]))
    yc = jnp.concatenate(yc_parts, axis=0)

    hc_ext[0:HALO, :] = hc_halo_ref[...] * keep_halo
    hc_ext[HALO:, :] = hc_ref[...]
    rows = 64
    first_off = HALO - (CONV_WIDTH - 1)
    yd_parts = []
    for r0 in range(0, tm, rows):
        acc = None
        for shift in range(SUBLANES):
            slab = None
            for off in range(shift, first_off + CONV_WIDTH, SUBLANES):
                j = off - first_off
                if j < 0:
                    continue
                base = r0 + off - shift
                if shift == 0:
                    term = hc_ext[base:base + rows, :] * convw_ref[j:j + 1, :]
                else:
                    term = hc_ext[base:base + rows + SUBLANES, :] * convw_ref[j:j + 1, :]
                slab = term if slab is None else slab + term
            part = slab if shift == 0 else slab[shift:shift + rows]
            acc = part if acc is None else acc + part
        yd_parts.append(acc)
    conv = jnp.concatenate(yd_parts, axis=0) + convb_ref[...]
    yd = _layer_norm(conv, cng_ref[...], cnb_ref[...])
    yd = yd * jax.nn.sigmoid(yd)

    x = x_ref[...]
    xn = _rms(x, g_ref[...]).astype(BF16)
    branch = (
        jnp.dot(ya.astype(BF16), wpa_ref[...], preferred_element_type=F32),
        lax.dot_general(ybT_ref[...], wpb_ref[...], (((0,), (0,)), ((), ())),
                        preferred_element_type=F32),
        jnp.dot(yc.astype(BF16), wpc_ref[...], preferred_element_type=F32),
        jnp.dot(yd.astype(BF16), wpd_ref[...], preferred_element_type=F32),
    )
    merged = jnp.zeros((tm, D_MODEL), F32)
    for b in range(N_BRANCH):
        cols = slice(b * D_MODEL, (b + 1) * D_MODEL)
        logit = jnp.dot(xn, wgate_ref[:, cols], preferred_element_type=F32) + bgate_ref[:, cols]
        merged = merged + jax.nn.sigmoid(logit) * branch[b]
    out_ref[...] = x + jnp.dot(merged.astype(BF16), wo_ref[...], preferred_element_type=F32)


def _mix(x2d, xa, zu, zv, hc, ybT, g, wpool, pscale, wsgu, bsgu, convw, convb, cng, cnb,
         wgate, bgate, wpa, wpb, wpc, wpd, wo, seq):
    n = x2d.shape[0]
    tm = TOKEN_TILE
    halo_blocks = tm // HALO
    tok = lambda w: pl.BlockSpec((tm, w), lambda i: (i, 0))
    halo = lambda w: pl.BlockSpec((HALO, w), lambda i: (jnp.maximum(i * halo_blocks - 1, 0), 0))
    consts = (g, wpool, pscale, wsgu, bsgu, convw, convb, cng, cnb, wgate, bgate,
              wpa, wpb, wpc, wpd, wo)
    return pl.pallas_call(
        functools.partial(_mix_kernel, tiles_per_seq=seq // tm),
        out_shape=jax.ShapeDtypeStruct((n, D_MODEL), F32),
        grid=(n // tm,),
        in_specs=[tok(D_MODEL), tok(POOL_DIM), halo(POOL_DIM), tok(SGU_DIM), tok(SGU_DIM),
                  tok(CONV_DIM), halo(CONV_DIM),
                  pl.BlockSpec((ATT_DIM, tm), lambda i: (0, i))]
                 + [_const_spec(a.shape) for a in consts],
        out_specs=tok(D_MODEL),
        scratch_shapes=[pltpu.VMEM((tm + HALO, POOL_DIM), F32),
                        pltpu.VMEM((tm + HALO, CONV_DIM), F32)],
        compiler_params=pltpu.CompilerParams(dimension_semantics=("parallel",),
                                             vmem_limit_bytes=VMEM_LIMIT),
        name="mix_merge",
    )(x2d, xa, xa, zu, zv, hc, hc, ybT, *consts)


def _ffn_kernel(x_ref, g_ref, wg_ref, wu_ref, wout_ref, gfin_ref, out_ref, *, final_norm):
    x = x_ref[...]
    hn = _rms(x, g_ref[...]).astype(BF16)
    y = x
    for c0 in range(0, FF_DIM, FF_CHUNK):
        cols = slice(c0, c0 + FF_CHUNK)
        gate = jnp.dot(hn, wg_ref[:, cols], preferred_element_type=F32)
        up = jnp.dot(hn, wu_ref[:, cols], preferred_element_type=F32)
        act = (gate * jax.nn.sigmoid(gate) * up).astype(BF16)
        y = y + jnp.dot(act, wout_ref[cols, :], preferred_element_type=F32)
    if final_norm:
        y = _rms(y, gfin_ref[...])
    out_ref[...] = y


def _ffn(x2d, g, wg, wu, wout, gfin, final_norm):
    n = x2d.shape[0]
    tm = TOKEN_TILE
    tok = pl.BlockSpec((tm, D_MODEL), lambda i: (i, 0))
    return pl.pallas_call(
        functools.partial(_ffn_kernel, final_norm=final_norm),
        out_shape=jax.ShapeDtypeStruct((n, D_MODEL), F32),
        grid=(n // tm,),
        in_specs=[tok, _const_spec((1, D_MODEL)), _const_spec(wg.shape), _const_spec(wu.shape),
                  _const_spec(wout.shape), _const_spec((1, D_MODEL))],
        out_specs=tok,
        compiler_params=pltpu.CompilerParams(dimension_semantics=("parallel",),
                                             vmem_limit_bytes=VMEM_LIMIT),
        name="swiglu_ffn",
    )(x2d, g, wg, wu, wout, gfin)


def _block_diag(blocks):
    g, r, c = blocks.shape
    out = jnp.zeros((g * r, g * c), blocks.dtype)
    for i in range(g):
        out = out.at[i * r:(i + 1) * r, i * c:(i + 1) * c].set(blocks[i])
    return out


def kernel(x, positions, norm_mix_g, w_in, w_gate, b_gate, pool_w, pool_scale, sgu_norm_g, sgu_norm_b, sgu_w, sgu_b, conv_w, conv_b, conv_norm_g, conv_norm_b, w_proj_a, w_proj_b, w_proj_c, w_proj_d, w_o, norm_ffn_g, w_ffn_in, w_ffn_out, final_norm_g):
    batch, seq, d = x.shape
    depth = w_in.shape[0]
    n = batch * seq
    assert d == D_MODEL and seq % TOKEN_TILE == 0 and TOKEN_TILE % MOBA_BLOCK == 0
    row = lambda v: v.reshape(1, -1).astype(F32)

    inv_freq = 1.0 / (ROPE_THETA ** (jnp.arange(HALF_DIM, dtype=F32) / HALF_DIM))
    cosT, sinT = _rope_tables(positions.astype(F32).reshape(1, n), inv_freq.reshape(HALF_DIM, 1))

    causal = jnp.tril(jnp.ones((SGU_CHUNK, SGU_CHUNK), dtype=bool))
    qkv_end = POOL_DIM + 3 * ATT_DIM
    x2d = x.reshape(n, d)
    for l in range(depth):
        wqkvT = w_in[l][:, POOL_DIM:qkv_end].T.astype(BF16)
        wrest = jnp.concatenate([w_in[l][:, :POOL_DIM], w_in[l][:, qkv_end:]], axis=1).astype(BF16)
        wsgu = jnp.where(causal[None], sgu_w[l], 0.0).reshape(SGU_GROUPS * SGU_CHUNK, SGU_CHUNK)
        bsgu = jnp.repeat(sgu_b[l].T, SGU_GROUP_DIM, axis=1)

        qT, kT, vT, xa, zu, zv, hc = _inproj(
            x2d, row(norm_mix_g[l]), wqkvT, wrest, cosT, sinT,
            row(sgu_norm_g[l]), row(sgu_norm_b[l]))
        ybT = _moba(qT, kT, vT, batch, seq)
        x2d = _mix(x2d, xa, zu, zv, hc, ybT, row(norm_mix_g[l]),
                   _block_diag(pool_w[l]).astype(BF16), row(pool_scale[l]),
                   wsgu.astype(BF16), bsgu.astype(F32), conv_w[l].astype(F32), row(conv_b[l]),
                   row(conv_norm_g[l]), row(conv_norm_b[l]),
                   w_gate[l].astype(BF16), row(b_gate[l]),
                   w_proj_a[l].astype(BF16), w_proj_b[l].astype(BF16),
                   w_proj_c[l].astype(BF16), w_proj_d[l].astype(BF16),
                   w_o[l].astype(BF16), seq)
        x2d = _ffn(x2d, row(norm_ffn_g[l]), w_ffn_in[l][:, :FF_DIM].astype(BF16),
                   w_ffn_in[l][:, FF_DIM:].astype(BF16), w_ffn_out[l].astype(BF16),
                   row(final_norm_g), final_norm=(l == depth - 1))
    return x2d.reshape(batch, seq, d)
```
